```python
import math
import jax
import jax.numpy as jnp
from jax import lax
import numpy as np

D_MODEL = 1024
BATCH = 4
SEQ = 4096
DEPTH = 4
DEC_BATCH = 128
DEC_SEQ = 8
PAST_LEN = 2048
PAGE_SIZE = 128

WINDOWS = (128, 512, 2048)
DILATIONS = (1, 4, 16)
N_GROUPS_A = 3
HEADS_PER_GROUP = 4
HEAD_DIM = 128
ATT_GROUP_W = HEADS_PER_GROUP * HEAD_DIM
ATT_W = N_GROUPS_A * ATT_GROUP_W
POOL_WINDOWS = (2, 4, 8, 16)
POOL_GROUPS = 4
POOL_GC = 128
POOL_W = POOL_GROUPS * POOL_GC
POOL_STATE = max(POOL_WINDOWS) - 1
CONV_W = 512
CONV_K = 31
CONV_STATE = CONV_K - 1
N_BRANCH = 3
D_IN = 3 * ATT_W + POOL_W + 2 * CONV_W + N_BRANCH * D_MODEL
D_FF = -(-8 * D_MODEL // (3 * 256)) * 256
EPS = 1e-6
NEG_INF = -1e30

kernel_name = "hybrid_dilated_pool_conformer_decode_step"


def _rmsnorm(x, g):
    xf = x.astype(jnp.float32)
    y = xf * lax.rsqrt(jnp.mean(xf * xf, axis=-1, keepdims=True) + EPS)
    return (y * g.astype(jnp.float32)).astype(x.dtype)


def _local_window_attn(q, k, v, n_back):
    n, L, h, dh = q.shape
    qb = math.gcd(L, 128)
    nblk = L // qb
    nb = n_back // qb
    kw = (nb + 1) * qb
    pad = ((0, 0), (n_back, 0), (0, 0), (0, 0))
    kp = jnp.pad(k, pad).reshape(n, nblk + nb, qb, h, dh)
    vp = jnp.pad(v, pad).reshape(n, nblk + nb, qb, h, dh)
    kwin = jnp.concatenate([kp[:, j:j + nblk] for j in range(nb + 1)], axis=2)
    vwin = jnp.concatenate([vp[:, j:j + nblk] for j in range(nb + 1)], axis=2)
    qblk = q.reshape(n, nblk, qb, h, dh)
    s = jnp.einsum('nbqhd,nbkhd->nbhqk', qblk, kwin, preferred_element_type=jnp.float32) * (dh ** -0.5)
    a = np.arange(qb)[None, :, None]
    m = np.arange(kw)[None, None, :]
    blk = np.arange(nblk)[:, None, None]
    dist = a + n_back - m
    valid = (dist >= 0) & (dist <= n_back) & (blk * qb + m - n_back >= 0)
    s = jnp.where(valid[None, :, None], s, NEG_INF)
    mx = jnp.max(s, axis=-1, keepdims=True)
    p = jnp.exp(s - mx)
    l = jnp.sum(p, axis=-1)
    o = jnp.einsum('nbhqk,nbkhd->nbqhd', p, vwin.astype(jnp.float32))
    o = o / jnp.swapaxes(l, -1, -2)[..., None]
    lse = jnp.swapaxes(mx[..., 0] + jnp.log(l), -1, -2)
    return o.reshape(n, L, h, dh), lse.reshape(n, L, h)


def _combine_groups(outs, lses):
    o = jnp.stack(outs)
    wts = jax.nn.softmax(jnp.stack(lses), axis=0)
    return jnp.einsum('gbth,gbthd->bthd', wts, o)


def _dilated_attn_prompt(q, k, v):
    b, s = q.shape[:2]
    outs, lses = [], []
    for g, (w, d) in enumerate(zip(WINDOWS, DILATIONS)):
        L = s // d

        def strided(t):
            t = t[:, :, g].reshape(b, L, d, HEADS_PER_GROUP, HEAD_DIM)
            return jnp.swapaxes(t, 1, 2).reshape(b * d, L, HEADS_PER_GROUP, HEAD_DIM)

        o, lse = _local_window_attn(strided(q), strided(k), strided(v), w // d)
        o = jnp.swapaxes(o.reshape(b, d, L, HEADS_PER_GROUP, HEAD_DIM), 1, 2).reshape(b, s, HEADS_PER_GROUP, HEAD_DIM)
        lse = jnp.swapaxes(lse.reshape(b, d, L, HEADS_PER_GROUP), 1, 2).reshape(b, s, HEADS_PER_GROUP)
        outs.append(o)
        lses.append(lse)
    return _combine_groups(outs, lses)


def _dilated_attn_sample(q, k, v, bufs):
    t = q.shape[1]
    outs, lses = [], []
    for g, (w, d) in enumerate(zip(WINDOWS, DILATIONS)):
        buf = bufs[g]
        wb = buf.shape[1]
        kc = jnp.concatenate([buf[:, :, 0], k[:, :, g]], axis=1)
        vc = jnp.concatenate([buf[:, :, 1], v[:, :, g]], axis=1)
        n_keys = w // d + 1
        idx = wb + np.arange(t)[:, None] - d * np.arange(n_keys)[None, :]
        valid = idx >= 0
        idx = np.maximum(idx, 0)
        kg = jnp.take(kc, idx, axis=1)
        vg = jnp.take(vc, idx, axis=1)
        s = jnp.einsum('bthd,btkhd->bhtk', q[:, :, g], kg, preferred_element_type=jnp.float32) * (HEAD_DIM ** -0.5)
        s = jnp.where(valid[None, None], s, NEG_INF)
        mx = jnp.max(s, axis=-1, keepdims=True)
        p = jnp.exp(s - mx)
        l = jnp.sum(p, axis=-1)
        o = jnp.einsum('bhtk,btkhd->bthd', p, vg.astype(jnp.float32)) / jnp.swapaxes(l, 1, 2)[..., None]
        outs.append(o)
        lses.append(jnp.swapaxes(mx[..., 0] + jnp.log(l), 1, 2))
    return _combine_groups(outs, lses)


def _pool_mixer(u, prev, pos0, w_grp, scale):
    b, t, c = u.shape
    p = prev.shape[1]
    buf = jnp.concatenate([prev.astype(u.dtype), u], axis=1).astype(jnp.float32)
    cs = jnp.concatenate([jnp.zeros((b, 1, c), jnp.float32), jnp.cumsum(buf, axis=1)], axis=1)
    hi = cs[:, p + 1:p + 1 + t]
    pos = pos0 + jnp.arange(t)
    pooled = []
    for gi, w in enumerate(POOL_WINDOWS):
        sl = slice(gi * POOL_GC, (gi + 1) * POOL_GC)
        lo = cs[:, p + 1 - w:p + 1 - w + t, sl]
        cnt = jnp.minimum(w, pos + 1).astype(jnp.float32)[None, :, None]
        pooled.append((hi[..., sl] - lo) / cnt)
    pooled = jnp.concatenate(pooled, axis=-1) - u.astype(jnp.float32)
    z = jnp.einsum('btgc,gcd->btgd', pooled.reshape(b, t, POOL_GROUPS, POOL_GC), w_grp.astype(jnp.float32))
    return (z.reshape(b, t, c) * scale.astype(jnp.float32)).astype(u.dtype)


def _conv_module(u_glu, prev, conv_w, conv_b, ln_g, ln_b):
    a, gt = jnp.split(u_glu, 2, axis=-1)
    u = a * jax.nn.sigmoid(gt)
    buf = jnp.concatenate([prev.astype(u.dtype), u], axis=1)
    y = lax.conv_general_dilated(buf, conv_w[:, None, :].astype(buf.dtype), window_strides=(1,), padding='VALID',
                                 dimension_numbers=('NWC', 'WIO', 'NWC'), feature_group_count=CONV_W)
    yf = y.astype(jnp.float32) + conv_b.astype(jnp.float32)
    mu = jnp.mean(yf, axis=-1, keepdims=True)
    var = jnp.mean(jnp.square(yf - mu), axis=-1, keepdims=True)
    yn = (yf - mu) * lax.rsqrt(var + EPS) * ln_g.astype(jnp.float32) + ln_b.astype(jnp.float32)
    return jax.nn.silu(yn).astype(u.dtype), buf[:, -CONV_STATE:]


def _layer(x, pos0, kv_bufs, prev_pool, prev_conv, w_in, w_att_o, w_pool_g, pool_scale, w_pool_o,
           conv_w, conv_b, ln_g, ln_b, w_conv_o, w_out, g_mix_pre, g_mix_post,
           w_ffn_in, w_ffn_down, g_ffn_pre, g_ffn_post):
    b, t, _ = x.shape
    h = _rmsnorm(x, g_mix_pre)
    proj = h @ w_in
    splits = [ATT_W, 2 * ATT_W, 3 * ATT_W, 3 * ATT_W + POOL_W, 3 * ATT_W + POOL_W + 2 * CONV_W]
    q, k, v, u_pool, u_glu, gates = jnp.split(proj, splits, axis=-1)
    hs = (b, t, N_GROUPS_A, HEADS_PER_GROUP, HEAD_DIM)
    q, k, v = q.reshape(hs), k.reshape(hs), v.reshape(hs)
    if kv_bufs is None:
        att = _dilated_attn_prompt(q, k, v)
        new_kv = [jnp.stack([k[:, -min(w, t):, g], v[:, -min(w, t):, g]], axis=2) for g, w in enumerate(WINDOWS)]
    else:
        att = _dilated_attn_sample(q, k, v, kv_bufs)
        new_kv = [jnp.stack([k[:, :, g], v[:, :, g]], axis=2) for g in range(N_GROUPS_A)]
    br_a = att.reshape(b, t, ATT_GROUP_W).astype(x.dtype) @ w_att_o
    br_b = _pool_mixer(u_pool, prev_pool, pos0, w_pool_g, pool_scale) @ w_pool_o
    new_pool = jnp.concatenate([prev_pool.astype(u_pool.dtype), u_pool], axis=1)[:, -POOL_STATE:]
    conv_out, new_conv = _conv_module(u_glu, prev_conv, conv_w, conv_b, ln_g, ln_b)
    br_c = conv_out @ w_conv_o
    gs = jax.nn.sigmoid(gates.reshape(b, t, N_BRANCH, D_MODEL))
    mixed = gs[:, :, 0] * br_a + gs[:, :, 1] * br_b + gs[:, :, 2] * br_c
    x = x + _rmsnorm(mixed @ w_out, g_mix_post)
    h2 = _rmsnorm(x, g_ffn_pre)
    gt, up = jnp.split(h2 @ w_ffn_in, 2, axis=-1)
    x = x + _rmsnorm((jax.nn.silu(gt) * up) @ w_ffn_down, g_ffn_post)
    return x, new_kv, new_pool, new_conv


def setup_inputs(seed: int = 0) -> dict:
    key = jax.random.key(seed)
    ks = jax.random.split(key, 32)

    def nrm(k, shape, scale):
        return jax.random.normal(k, shape, jnp.float32) * scale

    wb = [min(w, PAST_LEN) for w in WINDOWS]
    kv_tail = (2, HEADS_PER_GROUP, HEAD_DIM)
    return {
        'x_prompt': nrm(ks[0], (BATCH, SEQ, D_MODEL), 1.0),
        'x_sample': nrm(ks[1], (DEC_BATCH, DEC_SEQ, D_MODEL), 1.0),
        'cache_kv_w128': nrm(ks[2], (DEPTH, DEC_BATCH, wb[0]) + kv_tail, 1.0),
        'cache_kv_w512': nrm(ks[3], (DEPTH, DEC_BATCH, wb[1]) + kv_tail, 1.0),
        'cache_kv_w2048': nrm(ks[4], (DEPTH, DEC_BATCH, wb[2]) + kv_tail, 1.0),
        'state_pool': nrm(ks[5], (DEPTH, DEC_BATCH, POOL_STATE, POOL_W), 1.0),
        'state_conv': nrm(ks[6], (DEPTH, DEC_BATCH, CONV_STATE, CONV_W), 0.5),
        'w_in': nrm(ks[7], (DEPTH, D_MODEL, D_IN), D_MODEL ** -0.5),
        'w_att_o': nrm(ks[8], (DEPTH, ATT_GROUP_W, D_MODEL), ATT_GROUP_W ** -0.5),
        'w_pool_g': nrm(ks[9], (DEPTH, POOL_GROUPS, POOL_GC, POOL_GC), POOL_GC ** -0.5),
        'pool_scale': 1.0 + nrm(ks[10], (DEPTH, POOL_W), 0.1),
        'w_pool_o': nrm(ks[11], (DEPTH, POOL_W, D_MODEL), POOL_W ** -0.5),
        'conv_w': nrm(ks[12], (DEPTH, CONV_K, CONV_W), CONV_K ** -0.5),
        'conv_b': nrm(ks[13], (DEPTH, CONV_W), 0.02),
        'ln_g': 1.0 + nrm(ks[14], (DEPTH, CONV_W), 0.1),
        'ln_b': nrm(ks[15], (DEPTH, CONV_W), 0.02),
        'w_conv_o': nrm(ks[16], (DEPTH, CONV_W, D_MODEL), CONV_W ** -0.5),
        'w_out': nrm(ks[17], (DEPTH, D_MODEL, D_MODEL), D_MODEL ** -0.5),
        'g_mix_pre': 1.0 + nrm(ks[18], (DEPTH, D_MODEL), 0.1),
        'g_mix_post': 1.0 + nrm(ks[19], (DEPTH, D_MODEL), 0.1),
        'w_ffn_in': nrm(ks[20], (DEPTH, D_MODEL, 2 * D_FF), D_MODEL ** -0.5),
        'w_ffn_down': nrm(ks[21], (DEPTH, D_FF, D_MODEL), D_FF ** -0.5),
        'g_ffn_pre': 1.0 + nrm(ks[22], (DEPTH, D_MODEL), 0.1),
        'g_ffn_post': 1.0 + nrm(ks[23], (DEPTH, D_MODEL), 0.1),
    }


def reference(x_prompt, x_sample, cache_kv_w128, cache_kv_w512, cache_kv_w2048, state_pool, state_conv,
              w_in, w_att_o, w_pool_g, pool_scale, w_pool_o, conv_w, conv_b, ln_g, ln_b, w_conv_o, w_out,
              g_mix_pre, g_mix_post, w_ffn_in, w_ffn_down, g_ffn_pre, g_ffn_post):
    yp, ys = x_prompt, x_sample
    bp = x_prompt.shape[0]
    caches = (cache_kv_w128, cache_kv_w512, cache_kv_w2048)
    p_kv = ([], [], [])
    s_kv = ([], [], [])
    p_pool, p_conv, s_pool, s_conv = [], [], [], []
    for l in range(DEPTH):
        lw = (w_in[l], w_att_o[l], w_pool_g[l], pool_scale[l], w_pool_o[l], conv_w[l], conv_b[l],
              ln_g[l], ln_b[l], w_conv_o[l], w_out[l], g_mix_pre[l], g_mix_post[l],
              w_ffn_in[l], w_ffn_down[l], g_ffn_pre[l], g_ffn_post[l])
        zp = jnp.zeros((bp, POOL_STATE, POOL_W), x_prompt.dtype)
        zc = jnp.zeros((bp, CONV_STATE, CONV_W), x_prompt.dtype)
        yp, kvp, npool, nconv = _layer(yp, 0, None, zp, zc, *lw)
        ys, kvs, spl, scv = _layer(ys, PAST_LEN, [c[l] for c in caches], state_pool[l], state_conv[l], *lw)
        for g in range(N_GROUPS_A):
            p_kv[g].append(kvp[g])
            s_kv[g].append(kvs[g])
        p_pool.append(npool)
        p_conv.append(nconv)
        s_pool.append(spl)
        s_conv.append(scv)
    return (yp, ys,
            jnp.stack(p_kv[0]), jnp.stack(p_kv[1]), jnp.stack(p_kv[2]), jnp.stack(p_pool), jnp.stack(p_conv),
            jnp.stack(s_kv[0]), jnp.stack(s_kv[1]), jnp.stack(s_kv[2]), jnp.stack(s_pool), jnp.stack(s_conv))
```

```python
from functools import partial

import jax
import jax.numpy as jnp
from jax import lax
from jax.experimental import pallas as pl
from jax.experimental.pallas import tpu as pltpu

F32 = jnp.float32
BF16 = jnp.bfloat16

D_MODEL = 1024
DEPTH = 4
PAST_LEN = 2048
WINDOWS = (128, 512, 2048)
DILATIONS = (1, 4, 16)
N_GROUPS = 3
HEADS = 4
HEAD_DIM = 128
GROUP_W = HEADS * HEAD_DIM
ATT_W = N_GROUPS * GROUP_W
N_BACK = 128
POOL_WINDOWS = (2, 4, 8, 16)
POOL_GC = 128
POOL_W = 512
POOL_STATE = 15
CONV_W = 512
CONV_K = 31
CONV_STATE = 30
D_FF = 2816
EPS = 1e-6
NEG_INF = -1e30
SCALE = HEAD_DIM ** -0.5

QKV_W = 3 * ATT_W
UPG_W = POOL_W + 2 * CONV_W
GATE_W = 3 * D_MODEL

VMEM_LIMIT = 56 * 1024 * 1024

PROJ_TM = 1024
PROJ_TN = 1536
FFN_TM = 512
FFN_TF = 1408
ATT_P = 2048
MIX_P = 256
POOL_HALO = 16
CONV_HALO = 32
SAMPLE_ATT_BB = 2
SAMPLE_MIX_BB = 32


def _params(sem):
    return pltpu.CompilerParams(dimension_semantics=sem, vmem_limit_bytes=VMEM_LIMIT)


def _rms(x, g):
    ms = jnp.mean(x * x, axis=-1, keepdims=True)
    return x * lax.rsqrt(ms + EPS) * g


def _dot(a, b):
    return jnp.dot(a, b, preferred_element_type=F32)


def _dot_nt(a, b):
    return lax.dot_general(a, b, (((1,), (1,)), ((), ())), preferred_element_type=F32)


def _norm_proj_kernel(x_ref, g_ref, w_ref, o_ref, h_scr):
    @pl.when(pl.program_id(1) == 0)
    def _():
        h_scr[...] = _rms(x_ref[...], g_ref[...]).astype(BF16)

    o_ref[...] = _dot(h_scr[...], w_ref[...]).astype(o_ref.dtype)


def _norm_proj(x, g, w, layer, col0, width, out_dtype):
    t = x.shape[0]
    tm = min(PROJ_TM, t)
    nj = width // PROJ_TN
    cb0 = col0 // PROJ_TN
    return pl.pallas_call(
        _norm_proj_kernel,
        grid=(t // tm, nj),
        in_specs=[
            pl.BlockSpec((tm, D_MODEL), lambda i, j: (i, 0)),
            pl.BlockSpec((None, 1, D_MODEL), lambda i, j: (layer, 0, 0)),
            pl.BlockSpec((None, D_MODEL, PROJ_TN), lambda i, j: (layer, 0, cb0 + j)),
        ],
        out_specs=pl.BlockSpec((tm, PROJ_TN), lambda i, j: (i, j)),
        out_shape=jax.ShapeDtypeStruct((t, width), out_dtype),
        scratch_shapes=[pltpu.VMEM((tm, D_MODEL), BF16)],
        compiler_params=_params(("parallel", "arbitrary")),
        name="norm_proj",
    )(x, g, w)


def _ffn_kernel(x_ref, gpre_ref, gpost_ref, wg_ref, wu_ref, wd_ref, o_ref, h_scr, acc_scr):
    j = pl.program_id(1)

    @pl.when(j == 0)
    def _():
        h_scr[...] = _rms(x_ref[...], gpre_ref[...]).astype(BF16)

    h = h_scr[...]
    gt = _dot(h, wg_ref[...])
    up = _dot(h, wu_ref[...])
    act = (gt * jax.nn.sigmoid(gt) * up).astype(BF16)
    part = _dot(act, wd_ref[...])

    @pl.when(j == 0)
    def _():
        acc_scr[...] = part

    @pl.when(j > 0)
    def _():
        acc_scr[...] += part

    @pl.when(j == pl.num_programs(1) - 1)
    def _():
        o_ref[...] = x_ref[...] + _rms(acc_scr[...], gpost_ref[...])


def _ffn(x, gpre, gpost, w_in, w_down, layer):
    t = x.shape[0]
    tm = min(FFN_TM, t)
    nj = D_FF // FFN_TF
    return pl.pallas_call(
        _ffn_kernel,
        grid=(t // tm, nj),
        in_specs=[
            pl.BlockSpec((tm, D_MODEL), lambda i, j: (i, 0)),
            pl.BlockSpec((None, 1, D_MODEL), lambda i, j: (layer, 0, 0)),
            pl.BlockSpec((None, 1, D_MODEL), lambda i, j: (layer, 0, 0)),
            pl.BlockSpec((None, D_MODEL, FFN_TF), lambda i, j: (layer, 0, j)),
            pl.BlockSpec((None, D_MODEL, FFN_TF), lambda i, j: (layer, 0, nj + j)),
            pl.BlockSpec((None, FFN_TF, D_MODEL), lambda i, j: (layer, j, 0)),
        ],
        out_specs=pl.BlockSpec((tm, D_MODEL), lambda i, j: (i, 0)),
        out_shape=jax.ShapeDtypeStruct((t, D_MODEL), F32),
        scratch_shapes=[pltpu.VMEM((tm, D_MODEL), BF16), pltpu.VMEM((tm, D_MODEL), F32)],
        compiler_params=_params(("parallel", "arbitrary")),
        name="ffn",
    )(x, gpre, gpost, w_in, w_in, w_down)


def _attn_prompt_kernel(q0, q1, q2, k0, k1, k2, v0, v1, v2, kh0, kh1, kh2, vh0, vh1, vh2,
                        o_ref, qd, kd, vd, m_s, l_s, a_s, bias_s):
    q_refs, k_refs, v_refs = (q0, q1, q2), (k0, k1, k2), (v0, v1, v2)
    kh_refs, vh_refs = (kh0, kh1, kh2), (vh0, vh1, vh2)
    first_tile = pl.program_id(2) == 0
    p_rows = q0.shape[0]
    nb = N_BACK

    qi = lax.broadcasted_iota(jnp.int32, (nb, 2 * nb), 0)
    mi = lax.broadcasted_iota(jnp.int32, (nb, 2 * nb), 1)
    band = (mi >= qi) & (mi <= qi + nb)
    bias_s[0] = jnp.where(band, 0.0, NEG_INF)
    bias_s[1] = jnp.where(band & (mi >= nb), 0.0, NEG_INF)

    for g, d in enumerate(DILATIONS):
        n = p_rows // d
        stride = None if d == 1 else d
        for r in range(d):
            qd[g, r * n:(r + 1) * n, :] = q_refs[g][pl.ds(r, n, stride=stride), :].astype(BF16)
            base = r * (n + nb)
            kd[g, base:base + nb, :] = kh_refs[g][pl.ds(r, nb, stride=stride), :].astype(BF16)
            kd[g, base + nb:base + nb + n, :] = k_refs[g][pl.ds(r, n, stride=stride), :].astype(BF16)
            vd[g, base:base + nb, :] = vh_refs[g][pl.ds(r, nb, stride=stride), :].astype(BF16)
            vd[g, base + nb:base + nb + n, :] = v_refs[g][pl.ds(r, n, stride=stride), :].astype(BF16)

        spr = n // nb

        def unit(u, carry, g=g, d=d, spr=spr):
            r = u // spr
            s = u - r * spr
            qoff = pl.multiple_of(u * nb, nb)
            koff = pl.multiple_of((u + r) * nb, nb)
            q = qd[g, pl.ds(qoff, nb), :]
            kk = kd[g, pl.ds(koff, 2 * nb), :]
            vv = vd[g, pl.ds(koff, 2 * nb), :]
            first = jnp.logical_and(s == 0, first_tile).astype(jnp.int32)
            sc = _dot_nt(q, kk) * SCALE + bias_s[first]
            m = jnp.max(sc, axis=-1, keepdims=True)
            p = jnp.exp(sc - m)
            l = jnp.sum(p, axis=-1, keepdims=True)
            acc = _dot(p.astype(BF16), vv)
            if d == 1:
                idx = pl.ds(qoff, nb)
            else:
                idx = pl.ds(r + d * nb * s, nb, stride=d)
            m_s[g, idx, :] = jnp.broadcast_to(m, (nb, HEAD_DIM))
            l_s[g, idx, :] = jnp.broadcast_to(l, (nb, HEAD_DIM))
            a_s[g, idx, :] = acc
            return carry

        lax.fori_loop(0, p_rows // nb, unit, 0)

    def merge(c, carry):
        rows = pl.ds(pl.multiple_of(c * nb, nb), nb)
        ms = [m_s[g, rows, :] for g in range(N_GROUPS)]
        mm = jnp.maximum(jnp.maximum(ms[0], ms[1]), ms[2])
        num = jnp.zeros((nb, HEAD_DIM), F32)
        den = jnp.zeros((nb, HEAD_DIM), F32)
        for g in range(N_GROUPS):
            w = jnp.exp(ms[g] - mm)
            num = num + w * a_s[g, rows, :]
            den = den + w * l_s[g, rows, :]
        o_ref[rows, :] = (num / den).astype(o_ref.dtype)
        return carry

    lax.fori_loop(0, p_rows // nb, merge, 0)


def _attn_prompt(qkv, batch, seq):
    t = qkv.shape[0]
    p = ATT_P
    tiles = seq // p
    nh = ATT_W // HEAD_DIM

    def main_spec(sec, g):
        return pl.BlockSpec((p, HEAD_DIM), lambda b, h, i: (b * tiles + i, sec * nh + g * HEADS + h))

    def halo_spec(sec, g):
        rows = N_BACK * DILATIONS[g]
        per_tile = p // rows
        per_seq = seq // rows
        return pl.BlockSpec(
            (rows, HEAD_DIM),
            lambda b, h, i: (jnp.maximum(b * per_seq + i * per_tile - 1, 0), sec * nh + g * HEADS + h))

    in_specs = ([main_spec(0, g) for g in range(N_GROUPS)]
                + [main_spec(1, g) for g in range(N_GROUPS)]
                + [main_spec(2, g) for g in range(N_GROUPS)]
                + [halo_spec(1, g) for g in range(N_GROUPS)]
                + [halo_spec(2, g) for g in range(N_GROUPS)])
    kv_rows = p + N_BACK * max(DILATIONS)
    return pl.pallas_call(
        _attn_prompt_kernel,
        grid=(batch, HEADS, tiles),
        in_specs=in_specs,
        out_specs=pl.BlockSpec((p, HEAD_DIM), lambda b, h, i: (b * tiles + i, h)),
        out_shape=jax.ShapeDtypeStruct((t, GROUP_W), BF16),
        scratch_shapes=[
            pltpu.VMEM((N_GROUPS, p, HEAD_DIM), BF16),
            pltpu.VMEM((N_GROUPS, kv_rows, HEAD_DIM), BF16),
            pltpu.VMEM((N_GROUPS, kv_rows, HEAD_DIM), BF16),
            pltpu.VMEM((N_GROUPS, p, HEAD_DIM), F32),
            pltpu.VMEM((N_GROUPS, p, HEAD_DIM), F32),
            pltpu.VMEM((N_GROUPS, p, HEAD_DIM), F32),
            pltpu.VMEM((2, N_BACK, 2 * N_BACK), F32),
        ],
        compiler_params=_params(("parallel", "parallel", "arbitrary")),
        name="attn_prompt",
    )(*([qkv] * 15))


def _softmax_update(state, sc, vv):
    m_blk = jnp.max(sc, axis=-1, keepdims=True)
    if state is None:
        m_new = m_blk
        p = jnp.exp(sc - m_new)
        return m_new, jnp.sum(p, axis=-1, keepdims=True), _dot(p.astype(BF16), vv)
    m_old, l_old, acc_old = state
    m_new = jnp.maximum(m_old, m_blk)
    alpha = jnp.exp(m_old - m_new)
    p = jnp.exp(sc - m_new)
    l_new = alpha * l_old + jnp.sum(p, axis=-1, keepdims=True)
    return m_new, l_new, alpha * acc_old + _dot(p.astype(BF16), vv)


def _attn_sample_kernel(qkv_ref, c0_ref, c1_ref, c2_ref, o_ref):
    bb = c0_ref.shape[0]
    t_new = qkv_ref.shape[0] // bb
    res2 = c2_ref.shape[2] // (2 * GROUP_W)
    c_refs = (c0_ref, c1_ref)

    def mask_for(n, d, new):
        ti = lax.broadcasted_iota(jnp.int32, (t_new, n), 0)
        ci = lax.broadcasted_iota(jnp.int32, (t_new, n), 1)
        dist = ti - ci
        if new:
            return (dist >= 0) & ((dist & (d - 1)) == 0)
        return (ci >= ti) & (((ci - ti) & (d - 1)) == 0)

    for b in range(bb):
        rows = slice(b * t_new, (b + 1) * t_new)
        for h in range(HEADS):
            state = None
            for g, d in enumerate(DILATIONS):
                col = g * GROUP_W + h * HEAD_DIM
                q = qkv_ref[rows, col:col + HEAD_DIM].astype(BF16)
                kn = qkv_ref[rows, ATT_W + col:ATT_W + col + HEAD_DIM].astype(BF16)
                vn = qkv_ref[rows, 2 * ATT_W + col:2 * ATT_W + col + HEAD_DIM].astype(BF16)
                hk = slice(h * HEAD_DIM, (h + 1) * HEAD_DIM)
                hv = slice(GROUP_W + h * HEAD_DIM, GROUP_W + (h + 1) * HEAD_DIM)
                if g < 2:
                    kc = c_refs[g][b, :, hk].astype(BF16)
                    vc = c_refs[g][b, :, hv].astype(BF16)
                    sc = jnp.where(mask_for(kc.shape[0], d, False), _dot_nt(q, kc) * SCALE, NEG_INF)
                    state = _softmax_update(state, sc, vc)
                else:
                    ti = lax.broadcasted_iota(jnp.int32, (t_new, N_BACK), 0)
                    for r in range(res2):
                        off = r * 2 * GROUP_W
                        kc = c2_ref[b, :, off + h * HEAD_DIM:off + (h + 1) * HEAD_DIM].astype(BF16)
                        vc = c2_ref[b, :, off + GROUP_W + h * HEAD_DIM:
                                    off + GROUP_W + (h + 1) * HEAD_DIM].astype(BF16)
                        sc = jnp.where(ti == r, _dot_nt(q, kc) * SCALE, NEG_INF)
                        state = _softmax_update(state, sc, vc)
                sn = jnp.where(mask_for(t_new, d, True), _dot_nt(q, kn) * SCALE, NEG_INF)
                state = _softmax_update(state, sn, vn)
            _, l, acc = state
            o_ref[rows, h * HEAD_DIM:(h + 1) * HEAD_DIM] = acc / l


def _attn_sample(qkv, c0, c1, c2, layer, t_new):
    t = qkv.shape[0]
    nreq = t // t_new
    bb = SAMPLE_ATT_BB
    res2 = t_new
    return pl.pallas_call(
        _attn_sample_kernel,
        grid=(nreq // bb,),
        in_specs=[
            pl.BlockSpec((bb * t_new, QKV_W), lambda i: (i, 0)),
            pl.BlockSpec((None, bb, c0.shape[2], 2 * GROUP_W), lambda i: (layer, i, 0, 0)),
            pl.BlockSpec((None, bb, c1.shape[2], 2 * GROUP_W), lambda i: (layer, i, 0, 0)),
            pl.BlockSpec((None, bb, c2.shape[2], res2 * 2 * GROUP_W), lambda i: (layer, i, 0, 0)),
        ],
        out_specs=pl.BlockSpec((bb * t_new, GROUP_W), lambda i: (i, 0)),
        out_shape=jax.ShapeDtypeStruct((t, GROUP_W), F32),
        compiler_params=_params(("parallel",)),
        name="attn_sample",
    )(qkv, c0, c1, c2)


def _mix_tail(x, att, pooled, y, gates, w):
    (watt, wbd, pscale, wpo, cb, lng, lnb, wco, wout, gpost) = w
    br_a = _dot(att.astype(BF16), watt[...])
    z = _dot(pooled.astype(BF16), wbd[...]) * pscale[...]
    br_b = _dot(z.astype(BF16), wpo[...])
    yf = y + cb[...]
    mu = jnp.mean(yf, axis=-1, keepdims=True)
    yc = yf - mu
    var = jnp.mean(yc * yc, axis=-1, keepdims=True)
    yn = yc * lax.rsqrt(var + EPS) * lng[...] + lnb[...]
    conv_out = yn * jax.nn.sigmoid(yn)
    br_c = _dot(conv_out.astype(BF16), wco[...])
    gs = jax.nn.sigmoid(gates.astype(F32))
    mixed = (gs[:, :D_MODEL] * br_a + gs[:, D_MODEL:2 * D_MODEL] * br_b
             + gs[:, 2 * D_MODEL:] * br_c)
    m = _dot(mixed.astype(BF16), wout[...])
    return x + _rms(m, gpost[...])


def _mixer_prompt_kernel(x_ref, att_ref, up_ref, uph_ref, a_ref, gt_ref, ah_ref, gth_ref, gates_ref,
                         watt, wbd, pscale, wpo, cw, cb, lng, lnb, wco, wout, gpost,
                         o_ref, nconv_ref, pbuf, cbuf):
    i = pl.program_id(1)
    p_rows = x_ref.shape[0]
    keep = (i > 0).astype(F32)

    u = up_ref[...]
    pbuf[0:POOL_HALO, :] = uph_ref[...] * keep
    pbuf[POOL_HALO:POOL_HALO + p_rows, :] = u
    pos = i * p_rows + lax.broadcasted_iota(jnp.int32, (p_rows, 1), 0)
    parts = []
    for gi, w in enumerate(POOL_WINDOWS):
        cs = slice(gi * POOL_GC, (gi + 1) * POOL_GC)
        acc = u[:, cs]
        for j in range(1, w):
            acc = acc + pbuf[POOL_HALO - j:POOL_HALO - j + p_rows, cs]
        cnt = jnp.minimum(w, pos + 1).astype(F32)
        parts.append(acc / cnt - u[:, cs])
    pooled = jnp.concatenate(parts, axis=1)

    ug = a_ref[...] * jax.nn.sigmoid(gt_ref[...])
    cbuf[0:CONV_HALO, :] = ah_ref[...] * jax.nn.sigmoid(gth_ref[...]) * keep
    cbuf[CONV_HALO:CONV_HALO + p_rows, :] = ug
    off0 = CONV_HALO - CONV_STATE
    y = cw[0:1, :] * cbuf[off0:off0 + p_rows, :]
    for j in range(1, CONV_K):
        y = y + cw[j:j + 1, :] * cbuf[off0 + j:off0 + j + p_rows, :]

    @pl.when(i == pl.num_programs(1) - 1)
    def _():
        nconv_ref[...] = cbuf[p_rows:p_rows + CONV_HALO, :]

    o_ref[...] = _mix_tail(x_ref[...], att_ref[...], pooled, y, gates_ref[...],
                           (watt, wbd, pscale, wpo, cb, lng, lnb, wco, wout, gpost))


def _weight_specs(layer):
    def vec(width):
        return pl.BlockSpec((None, 1, width), lambda *_: (layer, 0, 0))

    def mat(rows, cols):
        return pl.BlockSpec((None, rows, cols), lambda *_: (layer, 0, 0))

    return vec, mat


def _mixer_prompt(x, att, upg, gates, wts, layer, batch, seq):
    t = x.shape[0]
    p = MIX_P
    tiles = seq // p
    vec, mat = _weight_specs(layer)

    def row(i_b, i_t):
        return i_b * tiles + i_t

    def halo(rows, colblk):
        per_tile = p // rows
        per_seq = seq // rows
        return pl.BlockSpec(
            (rows, POOL_W), lambda b, i: (jnp.maximum(b * per_seq + i * per_tile - 1, 0), colblk))

    in_specs = [
        pl.BlockSpec((p, D_MODEL), lambda b, i: (row(b, i), 0)),
        pl.BlockSpec((p, GROUP_W), lambda b, i: (row(b, i), 0)),
        pl.BlockSpec((p, POOL_W), lambda b, i: (row(b, i), 0)),
        halo(POOL_HALO, 0),
        pl.BlockSpec((p, CONV_W), lambda b, i: (row(b, i), 1)),
        pl.BlockSpec((p, CONV_W), lambda b, i: (row(b, i), 2)),
        halo(CONV_HALO, 1),
        halo(CONV_HALO, 2),
        pl.BlockSpec((p, GATE_W), lambda b, i: (row(b, i), 0)),
        mat(GROUP_W, D_MODEL), mat(POOL_W, POOL_W), vec(POOL_W), mat(POOL_W, D_MODEL),
        mat(CONV_K, CONV_W), vec(CONV_W), vec(CONV_W), vec(CONV_W), mat(CONV_W, D_MODEL),
        mat(D_MODEL, D_MODEL), vec(D_MODEL),
    ]
    return pl.pallas_call(
        _mixer_prompt_kernel,
        grid=(batch, tiles),
        in_specs=in_specs,
        out_specs=[
            pl.BlockSpec((p, D_MODEL), lambda b, i: (row(b, i), 0)),
            pl.BlockSpec((None, CONV_HALO, CONV_W), lambda b, i: (b, 0, 0)),
        ],
        out_shape=[
            jax.ShapeDtypeStruct((t, D_MODEL), F32),
            jax.ShapeDtypeStruct((batch, CONV_HALO, CONV_W), F32),
        ],
        scratch_shapes=[
            pltpu.VMEM((POOL_HALO + p, POOL_W), F32),
            pltpu.VMEM((CONV_HALO + p, CONV_W), F32),
        ],
        compiler_params=_params(("parallel", "arbitrary")),
        name="mixer_prompt",
    )(x, att, upg, upg, upg, upg, upg, upg, gates, *wts)


def _mixer_sample_kernel(x_ref, att_ref, up_ref, a_ref, gt_ref, gates_ref, sp_ref, sc_ref,
                         watt, wbd, pscale, wpo, cw, cb, lng, lnb, wco, wout, gpost,
                         o_ref, npool_ref, nconv_ref):
    t_new = x_ref.shape[0]
    bb = x_ref.shape[1]

    def rows(ref):
        return jnp.concatenate([ref[t] for t in range(t_new)], axis=0)

    new_p = [up_ref[t] for t in range(t_new)]
    hist_p = [sp_ref[j] for j in range(POOL_STATE)] + new_p
    pooled = []
    for t in range(t_new):
        parts = []
        for gi, w in enumerate(POOL_WINDOWS):
            cs = slice(gi * POOL_GC, (gi + 1) * POOL_GC)
            acc = new_p[t][:, cs]
            for j in range(1, w):
                acc = acc + hist_p[POOL_STATE + t - j][:, cs]
            cnt = float(min(w, PAST_LEN + t + 1))
            parts.append(acc / cnt - new_p[t][:, cs])
        pooled.append(jnp.concatenate(parts, axis=1))
    for j in range(POOL_STATE):
        npool_ref[j] = hist_p[t_new + j]

    hist_c = ([sc_ref[j] for j in range(CONV_STATE)]
              + [a_ref[t] * jax.nn.sigmoid(gt_ref[t]) for t in range(t_new)])
    ys = []
    for t in range(t_new):
        y = cw[0:1, :] * hist_c[t]
        for j in range(1, CONV_K):
            y = y + cw[j:j + 1, :] * hist_c[t + j]
        ys.append(y)
    for j in range(CONV_STATE):
        nconv_ref[j] = hist_c[t_new + j]

    out = _mix_tail(rows(x_ref), rows(att_ref), jnp.concatenate(pooled, axis=0),
                    jnp.concatenate(ys, axis=0), rows(gates_ref),
                    (watt, wbd, pscale, wpo, cb, lng, lnb, wco, wout, gpost))
    for t in range(t_new):
        o_ref[t] = out[t * bb:(t + 1) * bb]


def _mixer_sample(x, att, upg, gates, sp, sc, wts, layer):
    t_new, nreq, _ = x.shape
    bb = SAMPLE_MIX_BB
    vec, mat = _weight_specs(layer)

    def act(width, colblk=0):
        return pl.BlockSpec((t_new, bb, width), lambda i: (0, i, colblk))

    in_specs = [
        act(D_MODEL), act(GROUP_W), act(POOL_W, 0), act(CONV_W, 1), act(CONV_W, 2), act(GATE_W),
        pl.BlockSpec((None, POOL_STATE, bb, POOL_W), lambda i: (layer, 0, i, 0)),
        pl.BlockSpec((None, CONV_STATE, bb, CONV_W), lambda i: (layer, 0, i, 0)),
        mat(GROUP_W, D_MODEL), mat(POOL_W, POOL_W), vec(POOL_W), mat(POOL_W, D_MODEL),
        mat(CONV_K, CONV_W), vec(CONV_W), vec(CONV_W), vec(CONV_W), mat(CONV_W, D_MODEL),
        mat(D_MODEL, D_MODEL), vec(D_MODEL),
    ]
    return pl.pallas_call(
        _mixer_sample_kernel,
        grid=(nreq // bb,),
        in_specs=in_specs,
        out_specs=[
            act(D_MODEL),
            pl.BlockSpec((POOL_STATE, bb, POOL_W), lambda i: (0, i, 0)),
            pl.BlockSpec((CONV_STATE, bb, CONV_W), lambda i: (0, i, 0)),
        ],
        out_shape=[
            jax.ShapeDtypeStruct((t_new, nreq, D_MODEL), F32),
            jax.ShapeDtypeStruct((POOL_STATE, nreq, POOL_W), F32),
            jax.ShapeDtypeStruct((CONV_STATE, nreq, CONV_W), F32),
        ],
        compiler_params=_params(("parallel",)),
        name="mixer_sample",
    )(x, att, upg, upg, upg, gates, sp, sc, *wts)


def _kv_rows(qkv, g):
    k = qkv[..., ATT_W + g * GROUP_W:ATT_W + (g + 1) * GROUP_W]
    v = qkv[..., 2 * ATT_W + g * GROUP_W:2 * ATT_W + (g + 1) * GROUP_W]
    kv = jnp.stack([k, v], axis=-2)
    return kv.reshape(kv.shape[:-1] + (HEADS, HEAD_DIM))


def kernel(x_prompt, x_sample, cache_kv_w128, cache_kv_w512, cache_kv_w2048, state_pool, state_conv,
           w_in, w_att_o, w_pool_g, pool_scale, w_pool_o, conv_w, conv_b, ln_g, ln_b, w_conv_o, w_out,
           g_mix_pre, g_mix_post, w_ffn_in, w_ffn_down, g_ffn_pre, g_ffn_post):
    batch, seq, _ = x_prompt.shape
    nreq, t_new, _ = x_sample.shape
    depth = w_in.shape[0]

    w_in_b = w_in.astype(BF16)
    w_ffn_in_b = w_ffn_in.astype(BF16)
    w_ffn_down_b = w_ffn_down.astype(BF16)
    eye = jnp.eye(len(POOL_WINDOWS), dtype=F32)
    w_bd = (w_pool_g[:, :, :, None, :] * eye[None, :, None, :, None]).reshape(depth, POOL_W, POOL_W)

    def vec(a):
        return a.reshape(depth, 1, a.shape[-1])

    mix_w = (w_att_o.astype(BF16), w_bd.astype(BF16), vec(pool_scale), w_pool_o.astype(BF16),
             conv_w, vec(conv_b), vec(ln_g), vec(ln_b), w_conv_o.astype(BF16), w_out.astype(BF16),
             vec(g_mix_post))
    g_pre, g_fpre, g_fpost = vec(g_mix_pre), vec(g_ffn_pre), vec(g_ffn_post)

    c0 = cache_kv_w128.reshape(depth, nreq, cache_kv_w128.shape[2], 2 * GROUP_W)
    c1 = cache_kv_w512.reshape(depth, nreq, cache_kv_w512.shape[2], 2 * GROUP_W)
    d2 = DILATIONS[2]
    c2 = cache_kv_w2048.reshape(depth, nreq, cache_kv_w2048.shape[2] // d2, d2 * 2 * GROUP_W)
    sp_t = jnp.swapaxes(state_pool, 1, 2)
    sc_t = jnp.swapaxes(state_conv, 1, 2)

    xp = x_prompt.reshape(batch * seq, D_MODEL)
    xs = jnp.swapaxes(x_sample, 0, 1).reshape(t_new * nreq, D_MODEL)
    outs = {k: [] for k in ("pkv0", "pkv1", "pkv2", "ppool", "pconv",
                            "skv0", "skv1", "skv2", "spool", "sconv")}

    for l in range(depth):
        qkv = _norm_proj(xp, g_pre, w_in_b, l, 0, QKV_W, F32)
        upg = _norm_proj(xp, g_pre, w_in_b, l, QKV_W, UPG_W, F32)
        gates = _norm_proj(xp, g_pre, w_in_b, l, QKV_W + UPG_W, GATE_W, BF16)
        att = _attn_prompt(qkv, batch, seq)
        x1, nconv = _mixer_prompt(xp, att, upg, gates, mix_w, l, batch, seq)
        xp = _ffn(x1, g_fpre, g_fpost, w_ffn_in_b, w_ffn_down_b, l)
        qkv3 = qkv.reshape(batch, seq, QKV_W)
        for g, w in enumerate(WINDOWS):
            outs[f"pkv{g}"].append(_kv_rows(qkv3[:, seq - min(w, seq):], g))
        outs["ppool"].append(upg.reshape(batch, seq, UPG_W)[:, seq - POOL_STATE:, :POOL_W])
        outs["pconv"].append(nconv[:, CONV_HALO - CONV_STATE:])

        qkv_s = _norm_proj(xs, g_pre, w_in_b, l, 0, QKV_W, F32)
        upg_s = _norm_proj(xs, g_pre, w_in_b, l, QKV_W, UPG_W, F32)
        gates_s = _norm_proj(xs, g_pre, w_in_b, l, QKV_W + UPG_W, GATE_W, BF16)
        qkv_s3 = jnp.swapaxes(qkv_s.reshape(t_new, nreq, QKV_W), 0, 1)
        att_s = _attn_sample(qkv_s3.reshape(nreq * t_new, QKV_W), c0, c1, c2, l, t_new)
        att_s = jnp.swapaxes(att_s.reshape(nreq, t_new, GROUP_W), 0, 1)
        x1s, npool_s, nconv_s = _mixer_sample(
            xs.reshape(t_new, nreq, D_MODEL), att_s, upg_s.reshape(t_new, nreq, UPG_W),
            gates_s.reshape(t_new, nreq, GATE_W), sp_t, sc_t, mix_w, l)
        xs = _ffn(x1s.reshape(t_new * nreq, D_MODEL), g_fpre, g_fpost, w_ffn_in_b, w_ffn_down_b, l)
        for g in range(N_GROUPS):
            outs[f"skv{g}"].append(_kv_rows(qkv_s3, g))
        outs["spool"].append(jnp.swapaxes(npool_s, 0, 1))
        outs["sconv"].append(jnp.swapaxes(nconv_s, 0, 1))

    st = {k: jnp.stack(v) for k, v in outs.items()}
    return (xp.reshape(batch, seq, D_MODEL), jnp.swapaxes(xs.reshape(t_new, nreq, D_MODEL), 0, 1),
            st["pkv0"], st["pkv1"], st["pkv2"], st["ppool"], st["pconv"],
            st["skv0"], st["skv1"], st["skv2"], st["spool"], st["sconv"])
```

```python
from functools import partial

import jax
import jax.numpy as jnp
from jax import lax
from jax.experimental import pallas as pl
from jax.experimental.pallas import tpu as pltpu

F32 = jnp.float32
BF16 = jnp.bfloat16

D_MODEL = 1024
DEPTH = 4
PAST_LEN = 2048
WINDOWS = (128, 512, 2048)
DILATIONS = (1, 4, 16)
N_GROUPS = 3
HEADS = 4
HEAD_DIM = 128
GROUP_W = HEADS * HEAD_DIM
ATT_W = N_GROUPS * GROUP_W
N_BACK = 128
POOL_WINDOWS = (2, 4, 8, 16)
POOL_GC = 128
POOL_W = 512
POOL_STATE = 15
CONV_W = 512
CONV_K = 31
CONV_STATE = 30
D_FF = 2816
EPS = 1e-6
NEG_INF = -1e30
SCALE = HEAD_DIM ** -0.5

QKV_W = 3 * ATT_W
UPG_W = POOL_W + 2 * CONV_W
GATE_W = 3 * D_MODEL

VMEM_LIMIT = 56 * 1024 * 1024

PROJ_TM = 1024
PROJ_TN = 1536
FFN_TM = 512
FFN_TF = 1408
ATT_P = 2048
ATT_UNROLL = 8
MIX_P = 256
POOL_HALO = 16
CONV_HALO = 32
CONV_CHUNK = 32
SUBLANES = 8
SAMPLE_ATT_BB = 2
SAMPLE_MIX_BB = 32


def _params(sem):
    return pltpu.CompilerParams(dimension_semantics=sem, vmem_limit_bytes=VMEM_LIMIT)


def _rms(x, g):
    ms = jnp.mean(x * x, axis=-1, keepdims=True)
    return x * lax.rsqrt(ms + EPS) * g


def _dot(a, b):
    return jnp.dot(a, b, preferred_element_type=F32)


def _dot_nt(a, b):
    return lax.dot_general(a, b, (((1,), (1,)), ((), ())), preferred_element_type=F32)


def _div_pow2(x, n):
    assert n & (n - 1) == 0
    return lax.shift_right_arithmetic(x, n.bit_length() - 1)


def _mod_pow2(x, n):
    assert n & (n - 1) == 0
    return x & (n - 1)


QKV_BLKS = QKV_W // PROJ_TN
UPG_BLKS = UPG_W // PROJ_TN
GATE_BLKS = GATE_W // PROJ_TN


def _norm_proj_kernel(x_ref, g_ref, w_ref, qkv_ref, upg_ref, gates_ref, h_scr):
    j = pl.program_id(1)

    @pl.when(j == 0)
    def _():
        h_scr[...] = _rms(x_ref[...], g_ref[...]).astype(BF16)

    @pl.when(j < QKV_BLKS)
    def _():
        qkv_ref[...] = _dot(h_scr[...], w_ref[...])

    @pl.when((j >= QKV_BLKS) & (j < QKV_BLKS + UPG_BLKS))
    def _():
        upg_ref[...] = _dot(h_scr[...], w_ref[...])

    @pl.when(j >= QKV_BLKS + UPG_BLKS)
    def _():
        gates_ref[...] = _dot(h_scr[...], w_ref[...]).astype(gates_ref.dtype)


def _norm_proj(x, g, w, layer):
    t = x.shape[0]
    tm = min(PROJ_TM, t)
    nj = QKV_BLKS + UPG_BLKS + GATE_BLKS
    lo_u, lo_g = QKV_BLKS, QKV_BLKS + UPG_BLKS
    return pl.pallas_call(
        _norm_proj_kernel,
        grid=(t // tm, nj),
        in_specs=[
            pl.BlockSpec((tm, D_MODEL), lambda i, j: (i, 0)),
            pl.BlockSpec((None, 1, D_MODEL), lambda i, j: (layer, 0, 0)),
            pl.BlockSpec((None, D_MODEL, PROJ_TN), lambda i, j: (layer, 0, j)),
        ],
        out_specs=[
            pl.BlockSpec((tm, PROJ_TN), lambda i, j: (i, jnp.minimum(j, QKV_BLKS - 1))),
            pl.BlockSpec((tm, PROJ_TN), lambda i, j: (i, jnp.clip(j - lo_u, 0, UPG_BLKS - 1))),
            pl.BlockSpec((tm, PROJ_TN), lambda i, j: (i, jnp.clip(j - lo_g, 0, GATE_BLKS - 1))),
        ],
        out_shape=[
            jax.ShapeDtypeStruct((t, QKV_W), F32),
            jax.ShapeDtypeStruct((t, UPG_W), F32),
            jax.ShapeDtypeStruct((t, GATE_W), BF16),
        ],
        scratch_shapes=[pltpu.VMEM((tm, D_MODEL), BF16)],
        compiler_params=_params(("parallel", "arbitrary")),
        name="norm_proj",
    )(x, g, w)


def _ffn_kernel(x_ref, gpre_ref, gpost_ref, win_ref, wd_ref, o_ref):
    x = x_ref[...]
    h = _rms(x, gpre_ref[...]).astype(BF16)
    acc = None
    for c0 in range(0, D_FF, FFN_TF):
        gt = _dot(h, win_ref[:, c0:c0 + FFN_TF])
        up = _dot(h, win_ref[:, D_FF + c0:D_FF + c0 + FFN_TF])
        act = (gt * jax.nn.sigmoid(gt) * up).astype(BF16)
        part = _dot(act, wd_ref[c0:c0 + FFN_TF, :])
        acc = part if acc is None else acc + part
    o_ref[...] = x + _rms(acc, gpost_ref[...])


def _ffn(x, gpre, gpost, w_in, w_down, layer):
    t = x.shape[0]
    tm = min(FFN_TM, t)
    once = pl.Buffered(1)
    return pl.pallas_call(
        _ffn_kernel,
        grid=(t // tm,),
        in_specs=[
            pl.BlockSpec((tm, D_MODEL), lambda i: (i, 0)),
            pl.BlockSpec((None, 1, D_MODEL), lambda i: (layer, 0, 0)),
            pl.BlockSpec((None, 1, D_MODEL), lambda i: (layer, 0, 0)),
            pl.BlockSpec((None, D_MODEL, 2 * D_FF), lambda i: (layer, 0, 0), pipeline_mode=once),
            pl.BlockSpec((None, D_FF, D_MODEL), lambda i: (layer, 0, 0), pipeline_mode=once),
        ],
        out_specs=pl.BlockSpec((tm, D_MODEL), lambda i: (i, 0)),
        out_shape=jax.ShapeDtypeStruct((t, D_MODEL), F32),
        compiler_params=_params(("parallel",)),
        name="ffn",
    )(x, gpre, gpost, w_in, w_down)


def _attn_prompt_kernel(q0, q1, q2, k0, k1, k2, v0, v1, v2, kh0, kh1, kh2, vh0, vh1, vh2,
                        o_ref, qd, kd, vd, m_s, l_s, a_s, bias_s):
    q_refs, k_refs, v_refs = (q0, q1, q2), (k0, k1, k2), (v0, v1, v2)
    kh_refs, vh_refs = (kh0, kh1, kh2), (vh0, vh1, vh2)
    first_tile = pl.program_id(2) == 0
    p_rows = q0.shape[0]
    nb = N_BACK

    qi = lax.broadcasted_iota(jnp.int32, (nb, 2 * nb), 0)
    mi = lax.broadcasted_iota(jnp.int32, (nb, 2 * nb), 1)
    band = (mi >= qi) & (mi <= qi + nb)
    bias_s[0] = jnp.where(band, 0.0, NEG_INF)
    bias_s[1] = jnp.where(band & (mi >= nb), 0.0, NEG_INF)

    for g, d in enumerate(DILATIONS):
        n = p_rows // d
        stride = None if d == 1 else d
        for r in range(d):
            qd[g, r * n:(r + 1) * n, :] = q_refs[g][pl.ds(r, n, stride=stride), :].astype(BF16)
            base = r * (n + nb)
            kd[g, base:base + nb, :] = kh_refs[g][pl.ds(r, nb, stride=stride), :].astype(BF16)
            kd[g, base + nb:base + nb + n, :] = k_refs[g][pl.ds(r, n, stride=stride), :].astype(BF16)
            vd[g, base:base + nb, :] = vh_refs[g][pl.ds(r, nb, stride=stride), :].astype(BF16)
            vd[g, base + nb:base + nb + n, :] = v_refs[g][pl.ds(r, n, stride=stride), :].astype(BF16)

        spr = n // nb

        def unit(u, carry, g=g, d=d, spr=spr):
            r = u // spr
            s = u - r * spr
            qoff = pl.multiple_of(u * nb, nb)
            koff = pl.multiple_of((u + r) * nb, nb)
            q = qd[g, pl.ds(qoff, nb), :]
            kk = kd[g, pl.ds(koff, 2 * nb), :]
            vv = vd[g, pl.ds(koff, 2 * nb), :]
            first = jnp.logical_and(s == 0, first_tile).astype(jnp.int32)
            sc = _dot_nt(q, kk) * SCALE + bias_s[first]
            m = jnp.max(sc, axis=-1, keepdims=True)
            p = jnp.exp(sc - m)
            l = jnp.sum(p, axis=-1, keepdims=True)
            acc = _dot(p.astype(BF16), vv)
            if d == 1:
                idx = pl.ds(qoff, nb)
            else:
                idx = pl.ds(r + d * nb * s, nb, stride=d)
            m_s[g, idx, :] = jnp.broadcast_to(m, (nb, HEAD_DIM))
            l_s[g, idx, :] = jnp.broadcast_to(l, (nb, HEAD_DIM))
            a_s[g, idx, :] = acc
            return carry

        lax.fori_loop(0, p_rows // nb, unit, 0, unroll=ATT_UNROLL)

    def merge(c, carry):
        rows = pl.ds(pl.multiple_of(c * nb, nb), nb)
        ms = [m_s[g, rows, :] for g in range(N_GROUPS)]
        mm = jnp.maximum(jnp.maximum(ms[0], ms[1]), ms[2])
        num = jnp.zeros((nb, HEAD_DIM), F32)
        den = jnp.zeros((nb, HEAD_DIM), F32)
        for g in range(N_GROUPS):
            w = jnp.exp(ms[g] - mm)
            num = num + w * a_s[g, rows, :]
            den = den + w * l_s[g, rows, :]
        o_ref[rows, :] = (num / den).astype(o_ref.dtype)
        return carry

    lax.fori_loop(0, p_rows // nb, merge, 0)


def _attn_prompt(qkv, batch, seq):
    t = qkv.shape[0]
    p = ATT_P
    tiles = seq // p
    nh = ATT_W // HEAD_DIM

    def main_spec(sec, g):
        return pl.BlockSpec((p, HEAD_DIM), lambda b, h, i: (b * tiles + i, sec * nh + g * HEADS + h))

    def halo_spec(sec, g):
        rows = N_BACK * DILATIONS[g]
        per_tile = p // rows
        per_seq = seq // rows
        return pl.BlockSpec(
            (rows, HEAD_DIM),
            lambda b, h, i: (jnp.maximum(b * per_seq + i * per_tile - 1, 0), sec * nh + g * HEADS + h))

    in_specs = ([main_spec(0, g) for g in range(N_GROUPS)]
                + [main_spec(1, g) for g in range(N_GROUPS)]
                + [main_spec(2, g) for g in range(N_GROUPS)]
                + [halo_spec(1, g) for g in range(N_GROUPS)]
                + [halo_spec(2, g) for g in range(N_GROUPS)])
    kv_rows = p + N_BACK * max(DILATIONS)
    return pl.pallas_call(
        _attn_prompt_kernel,
        grid=(batch, HEADS, tiles),
        in_specs=in_specs,
        out_specs=pl.BlockSpec((p, HEAD_DIM), lambda b, h, i: (b * tiles + i, h)),
        out_shape=jax.ShapeDtypeStruct((t, GROUP_W), BF16),
        scratch_shapes=[
            pltpu.VMEM((N_GROUPS, p, HEAD_DIM), BF16),
            pltpu.VMEM((N_GROUPS, kv_rows, HEAD_DIM), BF16),
            pltpu.VMEM((N_GROUPS, kv_rows, HEAD_DIM), BF16),
            pltpu.VMEM((N_GROUPS, p, HEAD_DIM), F32),
            pltpu.VMEM((N_GROUPS, p, HEAD_DIM), F32),
            pltpu.VMEM((N_GROUPS, p, HEAD_DIM), F32),
            pltpu.VMEM((2, N_BACK, 2 * N_BACK), F32),
        ],
        compiler_params=_params(("parallel", "parallel", "arbitrary")),
        name="attn_prompt",
    )(*([qkv] * 15))


KV_ROWS = 2 * HEADS


def _attn_sample_kernel(qkv_ref, c0_ref, c1_ref, c2_ref, o_ref, bias0, bias1, bias2, biasn):
    bb = c0_ref.shape[0]
    t_new = qkv_ref.shape[0] // bb
    nrow = HEADS * t_new
    d1 = DILATIONS[1]

    @pl.when(pl.program_id(0) == 0)
    def _():
        def row_ids(ncol):
            shape = (HEADS, t_new, ncol)
            h = lax.broadcasted_iota(jnp.int32, shape, 0).reshape(nrow, ncol)
            t = lax.broadcasted_iota(jnp.int32, shape, 1).reshape(nrow, ncol)
            return h, t

        def cache_ids(ncol):
            col = lax.broadcasted_iota(jnp.int32, (nrow, ncol), 1)
            return _div_pow2(col, KV_ROWS), _mod_pow2(col, KV_ROWS)

        h, t = row_ids(bias0.shape[1])
        c, j = cache_ids(bias0.shape[1])
        bias0[...] = jnp.where((j == h) & (c >= t), 0.0, NEG_INF)
        h, t = row_ids(bias1.shape[1])
        c, j = cache_ids(bias1.shape[1])
        bias1[...] = jnp.where((j == h) & (c >= t) & (_mod_pow2(c - t, d1) == 0), 0.0, NEG_INF)
        h, t = row_ids(bias2.shape[2])
        _, j = cache_ids(bias2.shape[2])
        for r in range(t_new):
            bias2[r] = jnp.where((j == h) & (t == r), 0.0, NEG_INF)
        h, t = row_ids(nrow)
        col = lax.broadcasted_iota(jnp.int32, (nrow, nrow), 1)
        h2, t2 = _div_pow2(col, t_new), _mod_pow2(col, t_new)
        for g, d in enumerate(DILATIONS):
            biasn[g] = jnp.where((h2 == h) & (t2 <= t) & (_mod_pow2(t - t2, d) == 0), 0.0, NEG_INF)

    for b in range(bb):
        rows = slice(b * t_new, (b + 1) * t_new)

        def head_major(col0):
            return jnp.concatenate(
                [qkv_ref[rows, col0 + h * HEAD_DIM:col0 + (h + 1) * HEAD_DIM] for h in range(HEADS)],
                axis=0).astype(BF16)

        blocks = []
        for g in range(N_GROUPS):
            q = head_major(g * GROUP_W)
            if g == 0:
                cached = [(c0_ref[b], bias0[...])]
            elif g == 1:
                cached = [(c1_ref[b], bias1[...])]
            else:
                cached = [(c2_ref[b, :, r * KV_ROWS:(r + 1) * KV_ROWS, :].reshape(-1, HEAD_DIM), bias2[r])
                          for r in range(t_new)]
            for x, bias in cached:
                xb = x.astype(BF16)
                blocks.append((_dot_nt(q, xb) * SCALE + bias, xb, True))
            kn = head_major(ATT_W + g * GROUP_W)
            vn = head_major(2 * ATT_W + g * GROUP_W)
            blocks.append((_dot_nt(q, kn) * SCALE + biasn[g], vn, False))

        m = None
        for sc, _, _ in blocks:
            mb = jnp.max(sc, axis=-1, keepdims=True)
            m = mb if m is None else jnp.maximum(m, mb)
        l = jnp.zeros((nrow, 1), F32)
        acc = jnp.zeros((nrow, HEAD_DIM), F32)
        for sc, val, rotate in blocks:
            p = jnp.exp(sc - m)
            l = l + jnp.sum(p, axis=-1, keepdims=True)
            if rotate:
                p = pltpu.roll(p, HEADS, axis=1)
            acc = acc + _dot(p.astype(BF16), val)
        out = acc / l
        for h in range(HEADS):
            o_ref[rows, h * HEAD_DIM:(h + 1) * HEAD_DIM] = out[h * t_new:(h + 1) * t_new]


def _attn_sample(qkv, c0, c1, c2, layer, t_new):
    t = qkv.shape[0]
    nreq = t // t_new
    bb = SAMPLE_ATT_BB
    nrow = HEADS * t_new
    res_rows = t_new * KV_ROWS
    return pl.pallas_call(
        _attn_sample_kernel,
        grid=(nreq // bb,),
        in_specs=[
            pl.BlockSpec((bb * t_new, QKV_W), lambda i: (i, 0)),
            pl.BlockSpec((None, bb, c0.shape[2], HEAD_DIM), lambda i: (layer, i, 0, 0)),
            pl.BlockSpec((None, bb, c1.shape[2], HEAD_DIM), lambda i: (layer, i, 0, 0)),
            pl.BlockSpec((None, bb, c2.shape[2], res_rows, HEAD_DIM), lambda i: (layer, i, 0, 0, 0)),
        ],
        out_specs=pl.BlockSpec((bb * t_new, GROUP_W), lambda i: (i, 0)),
        out_shape=jax.ShapeDtypeStruct((t, GROUP_W), F32),
        scratch_shapes=[
            pltpu.VMEM((nrow, c0.shape[2]), F32),
            pltpu.VMEM((nrow, c1.shape[2]), F32),
            pltpu.VMEM((t_new, nrow, c2.shape[2] * KV_ROWS), F32),
            pltpu.VMEM((N_GROUPS, nrow, nrow), F32),
        ],
        compiler_params=_params(("arbitrary",)),
        name="attn_sample",
    )(qkv, c0, c1, c2)


def _mix_tail(x, att, pooled, y, gates, w):
    (watt, wbd, pscale, wpo, cb, lng, lnb, wco, wout, gpost) = w
    br_a = _dot(att.astype(BF16), watt[...])
    z = _dot(pooled.astype(BF16), wbd[...]) * pscale[...]
    br_b = _dot(z.astype(BF16), wpo[...])
    yf = y + cb[...]
    mu = jnp.mean(yf, axis=-1, keepdims=True)
    yc = yf - mu
    var = jnp.mean(yc * yc, axis=-1, keepdims=True)
    yn = yc * lax.rsqrt(var + EPS) * lng[...] + lnb[...]
    conv_out = yn * jax.nn.sigmoid(yn)
    br_c = _dot(conv_out.astype(BF16), wco[...])
    gs = jax.nn.sigmoid(gates.astype(F32))
    mixed = (gs[:, :D_MODEL] * br_a + gs[:, D_MODEL:2 * D_MODEL] * br_b
             + gs[:, 2 * D_MODEL:] * br_c)
    m = _dot(mixed.astype(BF16), wout[...])
    return x + _rms(m, gpost[...])


def _mixer_prompt_kernel(x_ref, att_ref, up_ref, uph_ref, a_ref, gt_ref, ah_ref, gth_ref, gates_ref,
                         watt, wbd, pscale, wpo, cw, cb, lng, lnb, wco, wout, gpost,
                         o_ref, nconv_ref, pbuf, cbuf, cshift):
    i = pl.program_id(1)
    p_rows = x_ref.shape[0]
    keep = (i > 0).astype(F32)

    u = up_ref[...]
    pbuf[0:POOL_HALO, :] = uph_ref[...] * keep
    pbuf[POOL_HALO:POOL_HALO + p_rows, :] = u
    pos = i * p_rows + lax.broadcasted_iota(jnp.int32, (p_rows, 1), 0)
    parts = []
    for gi, w in enumerate(POOL_WINDOWS):
        cs = slice(gi * POOL_GC, (gi + 1) * POOL_GC)
        acc = u[:, cs]
        for j in range(1, w):
            acc = acc + pbuf[POOL_HALO - j:POOL_HALO - j + p_rows, cs]
        cnt = jnp.minimum(w, pos + 1).astype(F32)
        parts.append(acc / cnt - u[:, cs])
    pooled = jnp.concatenate(parts, axis=1)

    ug = a_ref[...] * jax.nn.sigmoid(gt_ref[...])
    cbuf[0:CONV_HALO, :] = ah_ref[...] * jax.nn.sigmoid(gth_ref[...]) * keep
    cbuf[CONV_HALO:CONV_HALO + p_rows, :] = ug
    span = p_rows + CONV_HALO - SUBLANES
    for s in range(1, SUBLANES):
        cshift[s - 1, 0:span, :] = cbuf[s:s + span, :]
    off0 = CONV_HALO - CONV_STATE
    chunks = []
    for r0 in range(0, p_rows, CONV_CHUNK):
        yc = None
        for j in range(CONV_K):
            base, s = (off0 + j) // SUBLANES * SUBLANES + r0, (off0 + j) % SUBLANES
            win = (cbuf[base:base + CONV_CHUNK, :] if s == 0
                   else cshift[s - 1, base:base + CONV_CHUNK, :])
            term = cw[j:j + 1, :] * win
            yc = term if yc is None else yc + term
        chunks.append(yc)
    y = jnp.concatenate(chunks, axis=0)

    @pl.when(i == pl.num_programs(1) - 1)
    def _():
        nconv_ref[...] = cbuf[p_rows:p_rows + CONV_HALO, :]

    o_ref[...] = _mix_tail(x_ref[...], att_ref[...], pooled, y, gates_ref[...],
                           (watt, wbd, pscale, wpo, cb, lng, lnb, wco, wout, gpost))


def _weight_specs(layer):
    def vec(width):
        return pl.BlockSpec((None, 1, width), lambda *_: (layer, 0, 0))

    def mat(rows, cols):
        return pl.BlockSpec((None, rows, cols), lambda *_: (layer, 0, 0))

    return vec, mat


def _mixer_prompt(x, att, upg, gates, wts, layer, batch, seq):
    t = x.shape[0]
    p = MIX_P
    tiles = seq // p
    vec, mat = _weight_specs(layer)

    def row(i_b, i_t):
        return i_b * tiles + i_t

    def halo(rows, colblk):
        per_tile = p // rows
        per_seq = seq // rows
        return pl.BlockSpec(
            (rows, POOL_W), lambda b, i: (jnp.maximum(b * per_seq + i * per_tile - 1, 0), colblk))

    in_specs = [
        pl.BlockSpec((p, D_MODEL), lambda b, i: (row(b, i), 0)),
        pl.BlockSpec((p, GROUP_W), lambda b, i: (row(b, i), 0)),
        pl.BlockSpec((p, POOL_W), lambda b, i: (row(b, i), 0)),
        halo(POOL_HALO, 0),
        pl.BlockSpec((p, CONV_W), lambda b, i: (row(b, i), 1)),
        pl.BlockSpec((p, CONV_W), lambda b, i: (row(b, i), 2)),
        halo(CONV_HALO, 1),
        halo(CONV_HALO, 2),
        pl.BlockSpec((p, GATE_W), lambda b, i: (row(b, i), 0)),
        mat(GROUP_W, D_MODEL), mat(POOL_W, POOL_W), vec(POOL_W), mat(POOL_W, D_MODEL),
        mat(CONV_K, CONV_W), vec(CONV_W), vec(CONV_W), vec(CONV_W), mat(CONV_W, D_MODEL),
        mat(D_MODEL, D_MODEL), vec(D_MODEL),
    ]
    return pl.pallas_call(
        _mixer_prompt_kernel,
        grid=(batch, tiles),
        in_specs=in_specs,
        out_specs=[
            pl.BlockSpec((p, D_MODEL), lambda b, i: (row(b, i), 0)),
            pl.BlockSpec((None, CONV_HALO, CONV_W), lambda b, i: (b, 0, 0)),
        ],
        out_shape=[
            jax.ShapeDtypeStruct((t, D_MODEL), F32),
            jax.ShapeDtypeStruct((batch, CONV_HALO, CONV_W), F32),
        ],
        scratch_shapes=[
            pltpu.VMEM((POOL_HALO + p, POOL_W), F32),
            pltpu.VMEM((CONV_HALO + p, CONV_W), F32),
            pltpu.VMEM((SUBLANES - 1, CONV_HALO + p - SUBLANES, CONV_W), F32),
        ],
        compiler_params=_params(("parallel", "arbitrary")),
        name="mixer_prompt",
    )(x, att, upg, upg, upg, upg, upg, upg, gates, *wts)


def _mixer_sample_kernel(x_ref, att_ref, up_ref, a_ref, gt_ref, gates_ref, sp_ref, sc_ref,
                         watt, wbd, pscale, wpo, cw, cb, lng, lnb, wco, wout, gpost,
                         o_ref, npool_ref, nconv_ref):
    t_new = x_ref.shape[0]
    bb = x_ref.shape[1]

    def rows(ref):
        return jnp.concatenate([ref[t] for t in range(t_new)], axis=0)

    new_p = [up_ref[t] for t in range(t_new)]
    hist_p = [sp_ref[j] for j in range(POOL_STATE)] + new_p
    pooled = []
    for t in range(t_new):
        parts = []
        for gi, w in enumerate(POOL_WINDOWS):
            cs = slice(gi * POOL_GC, (gi + 1) * POOL_GC)
            acc = new_p[t][:, cs]
            for j in range(1, w):
                acc = acc + hist_p[POOL_STATE + t - j][:, cs]
            cnt = float(min(w, PAST_LEN + t + 1))
            parts.append(acc / cnt - new_p[t][:, cs])
        pooled.append(jnp.concatenate(parts, axis=1))
    for j in range(POOL_STATE):
        npool_ref[j] = hist_p[t_new + j]

    hist_c = ([sc_ref[j] for j in range(CONV_STATE)]
              + [a_ref[t] * jax.nn.sigmoid(gt_ref[t]) for t in range(t_new)])
    ys = []
    for t in range(t_new):
        y = cw[0:1, :] * hist_c[t]
        for j in range(1, CONV_K):
            y = y + cw[j:j + 1, :] * hist_c[t + j]
        ys.append(y)
    for j in range(CONV_STATE):
        nconv_ref[j] = hist_c[t_new + j]

    out = _mix_tail(rows(x_ref), rows(att_ref), jnp.concatenate(pooled, axis=0),
                    jnp.concatenate(ys, axis=0), rows(gates_ref),
                    (watt, wbd, pscale, wpo, cb, lng, lnb, wco, wout, gpost))
    for t in range(t_new):
        o_ref[t] = out[t * bb:(t + 1) * bb]


def _mixer_sample(x, att, upg, gates, sp, sc, wts, layer):
    t_new, nreq, _ = x.shape
    bb = SAMPLE_MIX_BB
    vec, mat = _weight_specs(layer)

    def act(width, colblk=0):
        return pl.BlockSpec((t_new, bb, width), lambda i: (0, i, colblk))

    in_specs = [
        act(D_MODEL), act(GROUP_W), act(POOL_W, 0), act(CONV_W, 1), act(CONV_W, 2), act(GATE_W),
        pl.BlockSpec((None, POOL_STATE, bb, POOL_W), lambda i: (layer, 0, i, 0)),
        pl.BlockSpec((None, CONV_STATE, bb, CONV_W), lambda i: (layer, 0, i, 0)),
        mat(GROUP_W, D_MODEL), mat(POOL_W, POOL_W), vec(POOL_W), mat(POOL_W, D_MODEL),
        mat(CONV_K, CONV_W), vec(CONV_W), vec(CONV_W), vec(CONV_W), mat(CONV_W, D_MODEL),
        mat(D_MODEL, D_MODEL), vec(D_MODEL),
    ]
    return pl.pallas_call(
        _mixer_sample_kernel,
        grid=(nreq // bb,),
        in_specs=in_specs,
        out_specs=[
            act(D_MODEL),
            pl.BlockSpec((POOL_STATE, bb, POOL_W), lambda i: (0, i, 0)),
            pl.BlockSpec((CONV_STATE, bb, CONV_W), lambda i: (0, i, 0)),
        ],
        out_shape=[
            jax.ShapeDtypeStruct((t_new, nreq, D_MODEL), F32),
            jax.ShapeDtypeStruct((POOL_STATE, nreq, POOL_W), F32),
            jax.ShapeDtypeStruct((CONV_STATE, nreq, CONV_W), F32),
        ],
        compiler_params=_params(("parallel",)),
        name="mixer_sample",
    )(x, att, upg, upg, upg, gates, sp, sc, *wts)


def _kv_rows(qkv, g):
    k = qkv[..., ATT_W + g * GROUP_W:ATT_W + (g + 1) * GROUP_W]
    v = qkv[..., 2 * ATT_W + g * GROUP_W:2 * ATT_W + (g + 1) * GROUP_W]
    kv = jnp.stack([k, v], axis=-2)
    return kv.reshape(kv.shape[:-1] + (HEADS, HEAD_DIM))


def kernel(x_prompt, x_sample, cache_kv_w128, cache_kv_w512, cache_kv_w2048, state_pool, state_conv,
           w_in, w_att_o, w_pool_g, pool_scale, w_pool_o, conv_w, conv_b, ln_g, ln_b, w_conv_o, w_out,
           g_mix_pre, g_mix_post, w_ffn_in, w_ffn_down, g_ffn_pre, g_ffn_post):
    batch, seq, _ = x_prompt.shape
    nreq, t_new, _ = x_sample.shape
    depth = w_in.shape[0]

    w_in_b = w_in.astype(BF16)
    w_ffn_in_b = w_ffn_in.astype(BF16)
    w_ffn_down_b = w_ffn_down.astype(BF16)
    eye = jnp.eye(len(POOL_WINDOWS), dtype=F32)
    w_bd = (w_pool_g[:, :, :, None, :] * eye[None, :, None, :, None]).reshape(depth, POOL_W, POOL_W)

    def vec(a):
        return a.reshape(depth, 1, a.shape[-1])

    mix_w = (w_att_o.astype(BF16), w_bd.astype(BF16), vec(pool_scale), w_pool_o.astype(BF16),
             conv_w, vec(conv_b), vec(ln_g), vec(ln_b), w_conv_o.astype(BF16), w_out.astype(BF16),
             vec(g_mix_post))
    g_pre, g_fpre, g_fpost = vec(g_mix_pre), vec(g_ffn_pre), vec(g_ffn_post)

    c0 = cache_kv_w128.reshape(depth, nreq, cache_kv_w128.shape[2] * KV_ROWS, HEAD_DIM)
    c1 = cache_kv_w512.reshape(depth, nreq, cache_kv_w512.shape[2] * KV_ROWS, HEAD_DIM)
    d2 = DILATIONS[2]
    c2 = cache_kv_w2048.reshape(depth, nreq, cache_kv_w2048.shape[2] // d2, d2 * KV_ROWS, HEAD_DIM)
    sp_t = jnp.swapaxes(state_pool, 1, 2)
    sc_t = jnp.swapaxes(state_conv, 1, 2)

    xp = x_prompt.reshape(batch * seq, D_MODEL)
    xs = jnp.swapaxes(x_sample, 0, 1).reshape(t_new * nreq, D_MODEL)
    outs = {k: [] for k in ("pkv0", "pkv1", "pkv2", "ppool", "pconv",
                            "skv0", "skv1", "skv2", "spool", "sconv")}

    for l in range(depth):
        qkv, upg, gates = _norm_proj(xp, g_pre, w_in_b, l)
        att = _attn_prompt(qkv, batch, seq)
        x1, nconv = _mixer_prompt(xp, att, upg, gates, mix_w, l, batch, seq)
        xp = _ffn(x1, g_fpre, g_fpost, w_ffn_in_b, w_ffn_down_b, l)
        qkv3 = qkv.reshape(batch, seq, QKV_W)
        for g, w in enumerate(WINDOWS):
            outs[f"pkv{g}"].append(_kv_rows(qkv3[:, seq - min(w, seq):], g))
        outs["ppool"].append(upg.reshape(batch, seq, UPG_W)[:, seq - POOL_STATE:, :POOL_W])
        outs["pconv"].append(nconv[:, CONV_HALO - CONV_STATE:])

        qkv_s, upg_s, gates_s = _norm_proj(xs, g_pre, w_in_b, l)
        qkv_s3 = jnp.swapaxes(qkv_s.reshape(t_new, nreq, QKV_W), 0, 1)
        att_s = _attn_sample(qkv_s3.reshape(nreq * t_new, QKV_W), c0, c1, c2, l, t_new)
        att_s = jnp.swapaxes(att_s.reshape(nreq, t_new, GROUP_W), 0, 1)
        x1s, npool_s, nconv_s = _mixer_sample(
            xs.reshape(t_new, nreq, D_MODEL), att_s, upg_s.reshape(t_new, nreq, UPG_W),
            gates_s.reshape(t_new, nreq, GATE_W), sp_t, sc_t, mix_w, l)
        xs = _ffn(x1s.reshape(t_new * nreq, D_MODEL), g_fpre, g_fpost, w_ffn_in_b, w_ffn_down_b, l)
        for g in range(N_GROUPS):
            outs[f"skv{g}"].append(_kv_rows(qkv_s3, g))
        outs["spool"].append(jnp.swapaxes(npool_s, 0, 1))
        outs["sconv"].append(jnp.swapaxes(nconv_s, 0, 1))

    st = {k: jnp.stack(v) for k, v in outs.items()}
    return (xp.reshape(batch, seq, D_MODEL), jnp.swapaxes(xs.reshape(t_new, nreq, D_MODEL), 0, 1),
            st["pkv0"], st["pkv1"], st["pkv2"], st["ppool"], st["pconv"],
            st["skv0"], st["skv1"], st["skv2"], st["spool"], st["sconv"])
```

```python
from functools import partial

import jax
import jax.numpy as jnp
from jax import lax
from jax.experimental import pallas as pl
from jax.experimental.pallas import tpu as pltpu

F32 = jnp.float32
BF16 = jnp.bfloat16

D_MODEL = 1024
DEPTH = 4
PAST_LEN = 2048
WINDOWS = (128, 512, 2048)
DILATIONS = (1, 4, 16)
N_GROUPS = 3
HEADS = 4
HEAD_DIM = 128
GROUP_W = HEADS * HEAD_DIM
ATT_W = N_GROUPS * GROUP_W
N_BACK = 128
POOL_WINDOWS = (2, 4, 8, 16)
POOL_GC = 128
POOL_W = 512
POOL_STATE = 15
CONV_W = 512
CONV_K = 31
CONV_STATE = 30
D_FF = 2816
EPS = 1e-6
NEG_INF = -1e30
SCALE = HEAD_DIM ** -0.5

QKV_W = 3 * ATT_W
UPG_W = POOL_W + 2 * CONV_W
UPC_W = POOL_W + CONV_W
GATE_W = 3 * D_MODEL

VMEM_LIMIT = 56 * 1024 * 1024

PROJ_TM = 1024
PROJ_TN = 1536
FFN_TM = 512
FFN_TF = 1408
ATT_P = 2048
MIX_P = 256
POOL_HALO = 16
CONV_HALO = 32
CONV_CHUNK = 32
SUBLANES = 8
SAMPLE_ATT_BB = 2
SAMPLE_MIX_BB = 32


def _params(sem):
    return pltpu.CompilerParams(dimension_semantics=sem, vmem_limit_bytes=VMEM_LIMIT)


def _rms(x, g):
    ms = jnp.mean(x * x, axis=-1, keepdims=True)
    return x * lax.rsqrt(ms + EPS) * g


def _dot(a, b):
    return jnp.dot(a, b, preferred_element_type=F32)


def _dot_nt(a, b):
    return lax.dot_general(a, b, (((1,), (1,)), ((), ())), preferred_element_type=F32)


def _div_pow2(x, n):
    assert n & (n - 1) == 0
    return lax.shift_right_arithmetic(x, n.bit_length() - 1)


def _mod_pow2(x, n):
    assert n & (n - 1) == 0
    return x & (n - 1)


QKV_BLKS = QKV_W // PROJ_TN
UPG_BLKS = UPG_W // PROJ_TN
GATE_BLKS = GATE_W // PROJ_TN


def _norm_proj_kernel(x_ref, g_ref, w_ref, qkv_ref, upc_ref, gates_ref, h_scr):
    j = pl.program_id(1)

    @pl.when(j == 0)
    def _():
        h_scr[...] = _rms(x_ref[...], g_ref[...]).astype(BF16)

    @pl.when(j < QKV_BLKS)
    def _():
        qkv_ref[...] = _dot(h_scr[...], w_ref[...])

    @pl.when((j >= QKV_BLKS) & (j < QKV_BLKS + UPG_BLKS))
    def _():
        r = _dot(h_scr[...], w_ref[...])
        upc_ref[:, :POOL_W] = r[:, :POOL_W]
        upc_ref[:, POOL_W:] = r[:, POOL_W:POOL_W + CONV_W] * jax.nn.sigmoid(r[:, POOL_W + CONV_W:])

    @pl.when(j >= QKV_BLKS + UPG_BLKS)
    def _():
        gates_ref[...] = jax.nn.sigmoid(_dot(h_scr[...], w_ref[...])).astype(gates_ref.dtype)


def _norm_proj(x, g, w, layer):
    t = x.shape[0]
    tm = min(PROJ_TM, t)
    nj = QKV_BLKS + UPG_BLKS + GATE_BLKS
    assert UPG_BLKS == 1
    lo_g = QKV_BLKS + UPG_BLKS
    return pl.pallas_call(
        _norm_proj_kernel,
        grid=(t // tm, nj),
        in_specs=[
            pl.BlockSpec((tm, D_MODEL), lambda i, j: (i, 0)),
            pl.BlockSpec((None, 1, D_MODEL), lambda i, j: (layer, 0, 0)),
            pl.BlockSpec((None, D_MODEL, PROJ_TN), lambda i, j: (layer, 0, j)),
        ],
        out_specs=[
            pl.BlockSpec((tm, PROJ_TN), lambda i, j: (i, jnp.minimum(j, QKV_BLKS - 1))),
            pl.BlockSpec((tm, UPC_W), lambda i, j: (i, 0)),
            pl.BlockSpec((tm, PROJ_TN), lambda i, j: (i, jnp.clip(j - lo_g, 0, GATE_BLKS - 1))),
        ],
        out_shape=[
            jax.ShapeDtypeStruct((t, QKV_W), F32),
            jax.ShapeDtypeStruct((t, UPC_W), F32),
            jax.ShapeDtypeStruct((t, GATE_W), BF16),
        ],
        scratch_shapes=[pltpu.VMEM((tm, D_MODEL), BF16)],
        compiler_params=_params(("parallel", "arbitrary")),
        name="norm_proj",
    )(x, g, w)


def _ffn_kernel(x_ref, gpre_ref, gpost_ref, win_ref, wd_ref, o_ref):
    x = x_ref[...]
    h = _rms(x, gpre_ref[...]).astype(BF16)
    acc = None
    for c0 in range(0, D_FF, FFN_TF):
        gt = _dot(h, win_ref[:, c0:c0 + FFN_TF])
        up = _dot(h, win_ref[:, D_FF + c0:D_FF + c0 + FFN_TF])
        act = (gt * jax.nn.sigmoid(gt) * up).astype(BF16)
        part = _dot(act, wd_ref[c0:c0 + FFN_TF, :])
        acc = part if acc is None else acc + part
    o_ref[...] = x + _rms(acc, gpost_ref[...])


def _ffn(x, gpre, gpost, w_in, w_down, layer):
    t = x.shape[0]
    tm = min(FFN_TM, t)
    once = pl.Buffered(1)
    return pl.pallas_call(
        _ffn_kernel,
        grid=(t // tm,),
        in_specs=[
            pl.BlockSpec((tm, D_MODEL), lambda i: (i, 0)),
            pl.BlockSpec((None, 1, D_MODEL), lambda i: (layer, 0, 0)),
            pl.BlockSpec((None, 1, D_MODEL), lambda i: (layer, 0, 0)),
            pl.BlockSpec((None, D_MODEL, 2 * D_FF), lambda i: (layer, 0, 0), pipeline_mode=once),
            pl.BlockSpec((None, D_FF, D_MODEL), lambda i: (layer, 0, 0), pipeline_mode=once),
        ],
        out_specs=pl.BlockSpec((tm, D_MODEL), lambda i: (i, 0)),
        out_shape=jax.ShapeDtypeStruct((t, D_MODEL), F32),
        compiler_params=_params(("parallel",)),
        name="ffn",
    )(x, gpre, gpost, w_in, w_down)


def _attn_prompt_kernel(q0, q1, q2, k0, k1, k2, v0, v1, v2, kh0, kh1, kh2, vh0, vh1, vh2,
                        o_ref, qd, kd, vd, m_s, l_s, a_s, bias_s):
    q_refs, k_refs, v_refs = (q0, q1, q2), (k0, k1, k2), (v0, v1, v2)
    kh_refs, vh_refs = (kh0, kh1, kh2), (vh0, vh1, vh2)
    first_tile = pl.program_id(2) == 0
    p_rows = q0.shape[0]
    nb = N_BACK

    qi = lax.broadcasted_iota(jnp.int32, (nb, 2 * nb), 0)
    mi = lax.broadcasted_iota(jnp.int32, (nb, 2 * nb), 1)
    band = (mi >= qi) & (mi <= qi + nb)
    bias_s[0] = jnp.where(band, 0.0, NEG_INF)
    bias_s[1] = jnp.where(band & (mi >= nb), 0.0, NEG_INF)

    for g, d in enumerate(DILATIONS):
        n = p_rows // d
        stride = None if d == 1 else d
        for r in range(d):
            qd[g, r * n:(r + 1) * n, :] = q_refs[g][pl.ds(r, n, stride=stride), :].astype(BF16)
            base = r * (n + nb)
            kd[g, base:base + nb, :] = kh_refs[g][pl.ds(r, nb, stride=stride), :].astype(BF16)
            kd[g, base + nb:base + nb + n, :] = k_refs[g][pl.ds(r, n, stride=stride), :].astype(BF16)
            vd[g, base:base + nb, :] = vh_refs[g][pl.ds(r, nb, stride=stride), :].astype(BF16)
            vd[g, base + nb:base + nb + n, :] = v_refs[g][pl.ds(r, n, stride=stride), :].astype(BF16)

        for r in range(d):
            for s in range(n // nb):
                qoff = r * n + s * nb
                koff = r * (n + nb) + s * nb
                q = qd[g, qoff:qoff + nb, :]
                kk = kd[g, koff:koff + 2 * nb, :]
                vv = vd[g, koff:koff + 2 * nb, :]
                bias = bias_s[first_tile.astype(jnp.int32)] if s == 0 else bias_s[0]
                sc = _dot_nt(q, kk) * SCALE + bias
                m = jnp.max(sc, axis=-1, keepdims=True)
                p = jnp.exp(sc - m)
                l = jnp.sum(p, axis=-1, keepdims=True)
                acc = _dot(p.astype(BF16), vv)
                idx = pl.ds(r + d * nb * s, nb, stride=stride)
                m_s[g, idx, :] = jnp.broadcast_to(m, (nb, HEAD_DIM))
                l_s[g, idx, :] = jnp.broadcast_to(l, (nb, HEAD_DIM))
                a_s[g, idx, :] = acc

    for c in range(p_rows // nb):
        rows = slice(c * nb, (c + 1) * nb)
        ms = [m_s[g, rows, :] for g in range(N_GROUPS)]
        mm = jnp.maximum(jnp.maximum(ms[0], ms[1]), ms[2])
        num = jnp.zeros((nb, HEAD_DIM), F32)
        den = jnp.zeros((nb, HEAD_DIM), F32)
        for g in range(N_GROUPS):
            w = jnp.exp(ms[g] - mm)
            num = num + w * a_s[g, rows, :]
            den = den + w * l_s[g, rows, :]
        o_ref[rows, :] = (num / den).astype(o_ref.dtype)


def _attn_prompt(qkv, batch, seq):
    t = qkv.shape[0]
    p = ATT_P
    tiles = seq // p
    nh = ATT_W // HEAD_DIM

    def main_spec(sec, g):
        return pl.BlockSpec((p, HEAD_DIM), lambda b, h, i: (b * tiles + i, sec * nh + g * HEADS + h))

    def halo_spec(sec, g):
        rows = N_BACK * DILATIONS[g]
        per_tile = p // rows
        per_seq = seq // rows
        return pl.BlockSpec(
            (rows, HEAD_DIM),
            lambda b, h, i: (jnp.maximum(b * per_seq + i * per_tile - 1, 0), sec * nh + g * HEADS + h))

    in_specs = ([main_spec(0, g) for g in range(N_GROUPS)]
                + [main_spec(1, g) for g in range(N_GROUPS)]
                + [main_spec(2, g) for g in range(N_GROUPS)]
                + [halo_spec(1, g) for g in range(N_GROUPS)]
                + [halo_spec(2, g) for g in range(N_GROUPS)])
    kv_rows = p + N_BACK * max(DILATIONS)
    return pl.pallas_call(
        _attn_prompt_kernel,
        grid=(batch, HEADS, tiles),
        in_specs=in_specs,
        out_specs=pl.BlockSpec((p, HEAD_DIM), lambda b, h, i: (b * tiles + i, h)),
        out_shape=jax.ShapeDtypeStruct((t, GROUP_W), BF16),
        scratch_shapes=[
            pltpu.VMEM((N_GROUPS, p, HEAD_DIM), BF16),
            pltpu.VMEM((N_GROUPS, kv_rows, HEAD_DIM), BF16),
            pltpu.VMEM((N_GROUPS, kv_rows, HEAD_DIM), BF16),
            pltpu.VMEM((N_GROUPS, p, HEAD_DIM), F32),
            pltpu.VMEM((N_GROUPS, p, HEAD_DIM), F32),
            pltpu.VMEM((N_GROUPS, p, HEAD_DIM), F32),
            pltpu.VMEM((2, N_BACK, 2 * N_BACK), F32),
        ],
        compiler_params=_params(("parallel", "parallel", "arbitrary")),
        name="attn_prompt",
    )(*([qkv] * 15))


KV_ROWS = 2 * HEADS


def _attn_sample_kernel(qkv_ref, c0_ref, c1_ref, c2_ref, o_ref, bias0, bias1, bias2, biasn):
    bb = c0_ref.shape[0]
    t_new = qkv_ref.shape[0] // bb
    nrow = HEADS * t_new
    d1 = DILATIONS[1]

    @pl.when(pl.program_id(0) == 0)
    def _():
        def row_ids(ncol):
            shape = (HEADS, t_new, ncol)
            h = lax.broadcasted_iota(jnp.int32, shape, 0).reshape(nrow, ncol)
            t = lax.broadcasted_iota(jnp.int32, shape, 1).reshape(nrow, ncol)
            return h, t

        def cache_ids(ncol):
            col = lax.broadcasted_iota(jnp.int32, (nrow, ncol), 1)
            return _div_pow2(col, KV_ROWS), _mod_pow2(col, KV_ROWS)

        h, t = row_ids(bias0.shape[1])
        c, j = cache_ids(bias0.shape[1])
        bias0[...] = jnp.where((j == h) & (c >= t), 0.0, NEG_INF)
        h, t = row_ids(bias1.shape[1])
        c, j = cache_ids(bias1.shape[1])
        bias1[...] = jnp.where((j == h) & (c >= t) & (_mod_pow2(c - t, d1) == 0), 0.0, NEG_INF)
        h, t = row_ids(bias2.shape[2])
        _, j = cache_ids(bias2.shape[2])
        for r in range(t_new):
            bias2[r] = jnp.where((j == h) & (t == r), 0.0, NEG_INF)
        h, t = row_ids(nrow)
        col = lax.broadcasted_iota(jnp.int32, (nrow, nrow), 1)
        h2, t2 = _div_pow2(col, t_new), _mod_pow2(col, t_new)
        for g, d in enumerate(DILATIONS):
            biasn[g] = jnp.where((h2 == h) & (t2 <= t) & (_mod_pow2(t - t2, d) == 0), 0.0, NEG_INF)

    for b in range(bb):
        rows = slice(b * t_new, (b + 1) * t_new)

        def head_major(col0):
            return jnp.concatenate(
                [qkv_ref[rows, col0 + h * HEAD_DIM:col0 + (h + 1) * HEAD_DIM] for h in range(HEADS)],
                axis=0).astype(BF16)

        blocks = []
        for g in range(N_GROUPS):
            q = head_major(g * GROUP_W)
            if g == 0:
                cached = [(c0_ref[b], bias0[...])]
            elif g == 1:
                cached = [(c1_ref[b], bias1[...])]
            else:
                cached = [(c2_ref[b, :, r * KV_ROWS:(r + 1) * KV_ROWS, :].reshape(-1, HEAD_DIM), bias2[r])
                          for r in range(t_new)]
            for x, bias in cached:
                xb = x.astype(BF16)
                blocks.append((_dot_nt(q, xb) * SCALE + bias, xb, True))
            kn = head_major(ATT_W + g * GROUP_W)
            vn = head_major(2 * ATT_W + g * GROUP_W)
            blocks.append((_dot_nt(q, kn) * SCALE + biasn[g], vn, False))

        m = None
        for sc, _, _ in blocks:
            mb = jnp.max(sc, axis=-1, keepdims=True)
            m = mb if m is None else jnp.maximum(m, mb)
        l = jnp.zeros((nrow, 1), F32)
        acc = jnp.zeros((nrow, HEAD_DIM), F32)
        for sc, val, rotate in blocks:
            p = jnp.exp(sc - m)
            l = l + jnp.sum(p, axis=-1, keepdims=True)
            if rotate:
                p = pltpu.roll(p, HEADS, axis=1)
            acc = acc + _dot(p.astype(BF16), val)
        out = acc / l
        for h in range(HEADS):
            o_ref[rows, h * HEAD_DIM:(h + 1) * HEAD_DIM] = out[h * t_new:(h + 1) * t_new]


def _attn_sample(qkv, c0, c1, c2, layer, t_new):
    t = qkv.shape[0]
    nreq = t // t_new
    bb = SAMPLE_ATT_BB
    nrow = HEADS * t_new
    res_rows = t_new * KV_ROWS
    return pl.pallas_call(
        _attn_sample_kernel,
        grid=(nreq // bb,),
        in_specs=[
            pl.BlockSpec((bb * t_new, QKV_W), lambda i: (i, 0)),
            pl.BlockSpec((None, bb, c0.shape[2], HEAD_DIM), lambda i: (layer, i, 0, 0)),
            pl.BlockSpec((None, bb, c1.shape[2], HEAD_DIM), lambda i: (layer, i, 0, 0)),
            pl.BlockSpec((None, bb, c2.shape[2], res_rows, HEAD_DIM), lambda i: (layer, i, 0, 0, 0)),
        ],
        out_specs=pl.BlockSpec((bb * t_new, GROUP_W), lambda i: (i, 0)),
        out_shape=jax.ShapeDtypeStruct((t, GROUP_W), F32),
        scratch_shapes=[
            pltpu.VMEM((nrow, c0.shape[2]), F32),
            pltpu.VMEM((nrow, c1.shape[2]), F32),
            pltpu.VMEM((t_new, nrow, c2.shape[2] * KV_ROWS), F32),
            pltpu.VMEM((N_GROUPS, nrow, nrow), F32),
        ],
        compiler_params=_params(("arbitrary",)),
        name="attn_sample",
    )(qkv, c0, c1, c2)


def _mix_tail(x, att, pooled, y, gates, w):
    (watt, wbd, pscale, wpo, cb, lng, lnb, wco, wout, gpost) = w
    br_a = _dot(att.astype(BF16), watt[...])
    z = _dot(pooled.astype(BF16), wbd[...]) * pscale[...]
    br_b = _dot(z.astype(BF16), wpo[...])
    yf = y + cb[...]
    mu = jnp.mean(yf, axis=-1, keepdims=True)
    yc = yf - mu
    var = jnp.mean(yc * yc, axis=-1, keepdims=True)
    yn = yc * lax.rsqrt(var + EPS) * lng[...] + lnb[...]
    conv_out = yn * jax.nn.sigmoid(yn)
    br_c = _dot(conv_out.astype(BF16), wco[...])
    gs = gates.astype(F32)
    mixed = (gs[:, :D_MODEL] * br_a + gs[:, D_MODEL:2 * D_MODEL] * br_b
             + gs[:, 2 * D_MODEL:] * br_c)
    m = _dot(mixed.astype(BF16), wout[...])
    return x + _rms(m, gpost[...])


def _mixer_prompt_kernel(x_ref, att_ref, up_ref, uph_ref, ug_ref, ugh_ref, gates_ref,
                         watt, wbd, pscale, wpo, cw, cb, lng, lnb, wco, wout, gpost,
                         o_ref, nconv_ref, pbuf, cbuf, cshift):
    i = pl.program_id(1)
    p_rows = x_ref.shape[0]
    keep = (i > 0).astype(F32)

    u = up_ref[...]
    pbuf[0:POOL_HALO, :] = uph_ref[...] * keep
    pbuf[POOL_HALO:POOL_HALO + p_rows, :] = u
    pos = i * p_rows + lax.broadcasted_iota(jnp.int32, (p_rows, 1), 0)
    parts = []
    for gi, w in enumerate(POOL_WINDOWS):
        cs = slice(gi * POOL_GC, (gi + 1) * POOL_GC)
        acc = u[:, cs]
        for j in range(1, w):
            acc = acc + pbuf[POOL_HALO - j:POOL_HALO - j + p_rows, cs]
        cnt = jnp.minimum(w, pos + 1).astype(F32)
        parts.append(acc / cnt - u[:, cs])
    pooled = jnp.concatenate(parts, axis=1)

    cbuf[0:CONV_HALO, :] = ugh_ref[...] * keep
    cbuf[CONV_HALO:CONV_HALO + p_rows, :] = ug_ref[...]
    span = p_rows + CONV_HALO - SUBLANES
    for s in range(1, SUBLANES):
        cshift[s - 1, 0:span, :] = cbuf[s:s + span, :]
    off0 = CONV_HALO - CONV_STATE
    chunks = []
    for r0 in range(0, p_rows, CONV_CHUNK):
        yc = None
        for j in range(CONV_K):
            base, s = (off0 + j) // SUBLANES * SUBLANES + r0, (off0 + j) % SUBLANES
            win = (cbuf[base:base + CONV_CHUNK, :] if s == 0
                   else cshift[s - 1, base:base + CONV_CHUNK, :])
            term = cw[j:j + 1, :] * win
            yc = term if yc is None else yc + term
        chunks.append(yc)
    y = jnp.concatenate(chunks, axis=0)

    @pl.when(i == pl.num_programs(1) - 1)
    def _():
        nconv_ref[...] = cbuf[p_rows:p_rows + CONV_HALO, :]

    o_ref[...] = _mix_tail(x_ref[...], att_ref[...], pooled, y, gates_ref[...],
                           (watt, wbd, pscale, wpo, cb, lng, lnb, wco, wout, gpost))


def _weight_specs(layer):
    def vec(width):
        return pl.BlockSpec((None, 1, width), lambda *_: (layer, 0, 0))

    def mat(rows, cols):
        return pl.BlockSpec((None, rows, cols), lambda *_: (layer, 0, 0))

    return vec, mat


def _mixer_prompt(x, att, upc, gates, wts, layer, batch, seq):
    t = x.shape[0]
    p = MIX_P
    tiles = seq // p
    vec, mat = _weight_specs(layer)

    def row(i_b, i_t):
        return i_b * tiles + i_t

    def halo(rows, colblk):
        per_tile = p // rows
        per_seq = seq // rows
        return pl.BlockSpec(
            (rows, POOL_W), lambda b, i: (jnp.maximum(b * per_seq + i * per_tile - 1, 0), colblk))

    in_specs = [
        pl.BlockSpec((p, D_MODEL), lambda b, i: (row(b, i), 0)),
        pl.BlockSpec((p, GROUP_W), lambda b, i: (row(b, i), 0)),
        pl.BlockSpec((p, POOL_W), lambda b, i: (row(b, i), 0)),
        halo(POOL_HALO, 0),
        pl.BlockSpec((p, CONV_W), lambda b, i: (row(b, i), 1)),
        halo(CONV_HALO, 1),
        pl.BlockSpec((p, GATE_W), lambda b, i: (row(b, i), 0)),
        mat(GROUP_W, D_MODEL), mat(POOL_W, POOL_W), vec(POOL_W), mat(POOL_W, D_MODEL),
        mat(CONV_K, CONV_W), vec(CONV_W), vec(CONV_W), vec(CONV_W), mat(CONV_W, D_MODEL),
        mat(D_MODEL, D_MODEL), vec(D_MODEL),
    ]
    return pl.pallas_call(
        _mixer_prompt_kernel,
        grid=(batch, tiles),
        in_specs=in_specs,
        out_specs=[
            pl.BlockSpec((p, D_MODEL), lambda b, i: (row(b, i), 0)),
            pl.BlockSpec((None, CONV_HALO, CONV_W), lambda b, i: (b, 0, 0)),
        ],
        out_shape=[
            jax.ShapeDtypeStruct((t, D_MODEL), F32),
            jax.ShapeDtypeStruct((batch, CONV_HALO, CONV_W), F32),
        ],
        scratch_shapes=[
            pltpu.VMEM((POOL_HALO + p, POOL_W), F32),
            pltpu.VMEM((CONV_HALO + p, CONV_W), F32),
            pltpu.VMEM((SUBLANES - 1, CONV_HALO + p - SUBLANES, CONV_W), F32),
        ],
        compiler_params=_params(("parallel", "arbitrary")),
        name="mixer_prompt",
    )(x, att, upc, upc, upc, upc, gates, *wts)


def _mixer_sample_kernel(x_ref, att_ref, up_ref, ug_ref, gates_ref, sp_ref, sc_ref,
                         watt, wbd, pscale, wpo, cw, cb, lng, lnb, wco, wout, gpost,
                         o_ref, npool_ref, nconv_ref):
    t_new = x_ref.shape[0]
    bb = x_ref.shape[1]

    def rows(ref):
        return jnp.concatenate([ref[t] for t in range(t_new)], axis=0)

    new_p = [up_ref[t] for t in range(t_new)]
    hist_p = [sp_ref[j] for j in range(POOL_STATE)] + new_p
    pooled = []
    for t in range(t_new):
        parts = []
        for gi, w in enumerate(POOL_WINDOWS):
            cs = slice(gi * POOL_GC, (gi + 1) * POOL_GC)
            acc = new_p[t][:, cs]
            for j in range(1, w):
                acc = acc + hist_p[POOL_STATE + t - j][:, cs]
            cnt = float(min(w, PAST_LEN + t + 1))
            parts.append(acc / cnt - new_p[t][:, cs])
        pooled.append(jnp.concatenate(parts, axis=1))
    for j in range(POOL_STATE):
        npool_ref[j] = hist_p[t_new + j]

    hist_c = [sc_ref[j] for j in range(CONV_STATE)] + [ug_ref[t] for t in range(t_new)]
    ys = []
    for t in range(t_new):
        y = cw[0:1, :] * hist_c[t]
        for j in range(1, CONV_K):
            y = y + cw[j:j + 1, :] * hist_c[t + j]
        ys.append(y)
    for j in range(CONV_STATE):
        nconv_ref[j] = hist_c[t_new + j]

    out = _mix_tail(rows(x_ref), rows(att_ref), jnp.concatenate(pooled, axis=0),
                    jnp.concatenate(ys, axis=0), rows(gates_ref),
                    (watt, wbd, pscale, wpo, cb, lng, lnb, wco, wout, gpost))
    for t in range(t_new):
        o_ref[t] = out[t * bb:(t + 1) * bb]


def _mixer_sample(x, att, upc, gates, sp, sc, wts, layer):
    t_new, nreq, _ = x.shape
    bb = SAMPLE_MIX_BB
    vec, mat = _weight_specs(layer)

    def act(width, colblk=0):
        return pl.BlockSpec((t_new, bb, width), lambda i: (0, i, colblk))

    in_specs = [
        act(D_MODEL), act(GROUP_W), act(POOL_W, 0), act(CONV_W, 1), act(GATE_W),
        pl.BlockSpec((None, POOL_STATE, bb, POOL_W), lambda i: (layer, 0, i, 0)),
        pl.BlockSpec((None, CONV_STATE, bb, CONV_W), lambda i: (layer, 0, i, 0)),
        mat(GROUP_W, D_MODEL), mat(POOL_W, POOL_W), vec(POOL_W), mat(POOL_W, D_MODEL),
        mat(CONV_K, CONV_W), vec(CONV_W), vec(CONV_W), vec(CONV_W), mat(CONV_W, D_MODEL),
        mat(D_MODEL, D_MODEL), vec(D_MODEL),
    ]
    return pl.pallas_call(
        _mixer_sample_kernel,
        grid=(nreq // bb,),
        in_specs=in_specs,
        out_specs=[
            act(D_MODEL),
            pl.BlockSpec((POOL_STATE, bb, POOL_W), lambda i: (0, i, 0)),
            pl.BlockSpec((CONV_STATE, bb, CONV_W), lambda i: (0, i, 0)),
        ],
        out_shape=[
            jax.ShapeDtypeStruct((t_new, nreq, D_MODEL), F32),
            jax.ShapeDtypeStruct((POOL_STATE, nreq, POOL_W), F32),
            jax.ShapeDtypeStruct((CONV_STATE, nreq, CONV_W), F32),
        ],
        compiler_params=_params(("parallel",)),
        name="mixer_sample",
    )(x, att, upc, upc, gates, sp, sc, *wts)


def _kv_rows(qkv, g):
    k = qkv[..., ATT_W + g * GROUP_W:ATT_W + (g + 1) * GROUP_W]
    v = qkv[..., 2 * ATT_W + g * GROUP_W:2 * ATT_W + (g + 1) * GROUP_W]
    kv = jnp.stack([k, v], axis=-2)
    return kv.reshape(kv.shape[:-1] + (HEADS, HEAD_DIM))


def kernel(x_prompt, x_sample, cache_kv_w128, cache_kv_w512, cache_kv_w2048, state_pool, state_conv,
           w_in, w_att_o, w_pool_g, pool_scale, w_pool_o, conv_w, conv_b, ln_g, ln_b, w_conv_o, w_out,
           g_mix_pre, g_mix_post, w_ffn_in, w_ffn_down, g_ffn_pre, g_ffn_post):
    batch, seq, _ = x_prompt.shape
    nreq, t_new, _ = x_sample.shape
    depth = w_in.shape[0]

    w_in_b = w_in.astype(BF16)
    w_ffn_in_b = w_ffn_in.astype(BF16)
    w_ffn_down_b = w_ffn_down.astype(BF16)
    eye = jnp.eye(len(POOL_WINDOWS), dtype=F32)
    w_bd = (w_pool_g[:, :, :, None, :] * eye[None, :, None, :, None]).reshape(depth, POOL_W, POOL_W)

    def vec(a):
        return a.reshape(depth, 1, a.shape[-1])

    mix_w = (w_att_o.astype(BF16), w_bd.astype(BF16), vec(pool_scale), w_pool_o.astype(BF16),
             conv_w, vec(conv_b), vec(ln_g), vec(ln_b), w_conv_o.astype(BF16), w_out.astype(BF16),
             vec(g_mix_post))
    g_pre, g_fpre, g_fpost = vec(g_mix_pre), vec(g_ffn_pre), vec(g_ffn_post)

    c0 = cache_kv_w128.reshape(depth, nreq, cache_kv_w128.shape[2] * KV_ROWS, HEAD_DIM)
    c1 = cache_kv_w512.reshape(depth, nreq, cache_kv_w512.shape[2] * KV_ROWS, HEAD_DIM)
    d2 = DILATIONS[2]
    c2 = cache_kv_w2048.reshape(depth, nreq, cache_kv_w2048.shape[2] // d2, d2 * KV_ROWS, HEAD_DIM)
    sp_t = jnp.swapaxes(state_pool, 1, 2)
    sc_t = jnp.swapaxes(state_conv, 1, 2)

    xp = x_prompt.reshape(batch * seq, D_MODEL)
    xs = jnp.swapaxes(x_sample, 0, 1).reshape(t_new * nreq, D_MODEL)
    outs = {k: [] for k in ("pkv0", "pkv1", "pkv2", "ppool", "pconv",
                            "skv0", "skv1", "skv2", "spool", "sconv")}

    for l in range(depth):
        qkv, upc, gates = _norm_proj(xp, g_pre, w_in_b, l)
        att = _attn_prompt(qkv, batch, seq)
        x1, nconv = _mixer_prompt(xp, att, upc, gates, mix_w, l, batch, seq)
        xp = _ffn(x1, g_fpre, g_fpost, w_ffn_in_b, w_ffn_down_b, l)
        qkv3 = qkv.reshape(batch, seq, QKV_W)
        for g, w in enumerate(WINDOWS):
            outs[f"pkv{g}"].append(_kv_rows(qkv3[:, seq - min(w, seq):], g))
        outs["ppool"].append(upc.reshape(batch, seq, UPC_W)[:, seq - POOL_STATE:, :POOL_W])
        outs["pconv"].append(nconv[:, CONV_HALO - CONV_STATE:])

        qkv_s, upc_s, gates_s = _norm_proj(xs, g_pre, w_in_b, l)
        qkv_s3 = jnp.swapaxes(qkv_s.reshape(t_new, nreq, QKV_W), 0, 1)
        att_s = _attn_sample(qkv_s3.reshape(nreq * t_new, QKV_W), c0, c1, c2, l, t_new)
        att_s = jnp.swapaxes(att_s.reshape(nreq, t_new, GROUP_W), 0, 1)
        x1s, npool_s, nconv_s = _mixer_sample(
            xs.reshape(t_new, nreq, D_MODEL), att_s, upc_s.reshape(t_new, nreq, UPC_W),
            gates_s.reshape(t_new, nreq, GATE_W), sp_t, sc_t, mix_w, l)
        xs = _ffn(x1s.reshape(t_new * nreq, D_MODEL), g_fpre, g_fpost, w_ffn_in_b, w_ffn_down_b, l)
        for g in range(N_GROUPS):
            outs[f"skv{g}"].append(_kv_rows(qkv_s3, g))
        outs["spool"].append(jnp.swapaxes(npool_s, 0, 1))
        outs["sconv"].append(jnp.swapaxes(nconv_s, 0, 1))

    st = {k: jnp.stack(v) for k, v in outs.items()}
    return (xp.reshape(batch, seq, D_MODEL), jnp.swapaxes(xs.reshape(t_new, nreq, D_MODEL), 0, 1),
            st["pkv0"], st["pkv1"], st["pkv2"], st["ppool"], st["pconv"],
            st["skv0"], st["skv1"], st["skv2"], st["spool"], st["sconv"])
```

```python
from functools import partial

import jax
import jax.numpy as jnp
from jax import lax
from jax.experimental import pallas as pl
from jax.experimental.pallas import tpu as pltpu

F32 = jnp.float32
BF16 = jnp.bfloat16

D_MODEL = 1024
DEPTH = 4
PAST_LEN = 2048
WINDOWS = (128, 512, 2048)
DILATIONS = (1, 4, 16)
N_GROUPS = 3
HEADS = 4
HEAD_DIM = 128
GROUP_W = HEADS * HEAD_DIM
ATT_W = N_GROUPS * GROUP_W
N_BACK = 128
POOL_WINDOWS = (2, 4, 8, 16)
POOL_GC = 128
POOL_W = 512
POOL_STATE = 15
CONV_W = 512
CONV_K = 31
CONV_STATE = 30
D_FF = 2816
EPS = 1e-6
NEG_INF = -1e30
SCALE = HEAD_DIM ** -0.5

QKV_W = 3 * ATT_W
UPG_W = POOL_W + 2 * CONV_W
UPC_W = POOL_W + CONV_W
GATE_W = 3 * D_MODEL

VMEM_LIMIT = 56 * 1024 * 1024

PROJ_TM = 1024
PROJ_TN = 1536
FFN_TM = 512
FFN_TF = 1408
ATT_P = 2048
MIX_P = 512
POOL_HALO = 16
CONV_HALO = 32
CONV_CHUNK = 32
SUBLANES = 8
SPLIT_STRIDE = 4
SAMPLE_ATT_BB = 2
SAMPLE_MIX_BB = 32


def _params(sem):
    return pltpu.CompilerParams(dimension_semantics=sem, vmem_limit_bytes=VMEM_LIMIT)


def _rms(x, g):
    ms = jnp.mean(x * x, axis=-1, keepdims=True)
    return x * lax.rsqrt(ms + EPS) * g


def _dot(a, b):
    return jnp.dot(a, b, preferred_element_type=F32)


def _dot_nt(a, b):
    return lax.dot_general(a, b, (((1,), (1,)), ((), ())), preferred_element_type=F32)


def _div_pow2(x, n):
    assert n & (n - 1) == 0
    return lax.shift_right_arithmetic(x, n.bit_length() - 1)


def _mod_pow2(x, n):
    assert n & (n - 1) == 0
    return x & (n - 1)


QKV_BLKS = QKV_W // PROJ_TN
UPG_BLKS = UPG_W // PROJ_TN
GATE_BLKS = GATE_W // PROJ_TN


def _norm_proj_kernel(x_ref, g_ref, w_ref, qkv_ref, upc_ref, gates_ref, h_scr):
    j = pl.program_id(1)

    @pl.when(j == 0)
    def _():
        h_scr[...] = _rms(x_ref[...], g_ref[...]).astype(BF16)

    @pl.when(j < QKV_BLKS)
    def _():
        r = _dot(h_scr[...], w_ref[...])
        for hc in range(qkv_ref.shape[0]):
            qkv_ref[hc] = r[:, hc * HEAD_DIM:(hc + 1) * HEAD_DIM]

    @pl.when((j >= QKV_BLKS) & (j < QKV_BLKS + UPG_BLKS))
    def _():
        r = _dot(h_scr[...], w_ref[...])
        upc_ref[:, :POOL_W] = r[:, :POOL_W]
        upc_ref[:, POOL_W:] = r[:, POOL_W:POOL_W + CONV_W] * jax.nn.sigmoid(r[:, POOL_W + CONV_W:])

    @pl.when(j >= QKV_BLKS + UPG_BLKS)
    def _():
        gates_ref[...] = jax.nn.sigmoid(_dot(h_scr[...], w_ref[...])).astype(gates_ref.dtype)


def _norm_proj(x, g, w, layer):
    t = x.shape[0]
    tm = min(PROJ_TM, t)
    nj = QKV_BLKS + UPG_BLKS + GATE_BLKS
    assert UPG_BLKS == 1
    lo_g = QKV_BLKS + UPG_BLKS
    return pl.pallas_call(
        _norm_proj_kernel,
        grid=(t // tm, nj),
        in_specs=[
            pl.BlockSpec((tm, D_MODEL), lambda i, j: (i, 0)),
            pl.BlockSpec((None, 1, D_MODEL), lambda i, j: (layer, 0, 0)),
            pl.BlockSpec((None, D_MODEL, PROJ_TN), lambda i, j: (layer, 0, j)),
        ],
        out_specs=[
            pl.BlockSpec((PROJ_TN // HEAD_DIM, tm, HEAD_DIM),
                         lambda i, j: (jnp.minimum(j, QKV_BLKS - 1), i, 0)),
            pl.BlockSpec((tm, UPC_W), lambda i, j: (i, 0)),
            pl.BlockSpec((tm, PROJ_TN), lambda i, j: (i, jnp.clip(j - lo_g, 0, GATE_BLKS - 1))),
        ],
        out_shape=[
            jax.ShapeDtypeStruct((QKV_W // HEAD_DIM, t, HEAD_DIM), F32),
            jax.ShapeDtypeStruct((t, UPC_W), F32),
            jax.ShapeDtypeStruct((t, GATE_W), BF16),
        ],
        scratch_shapes=[pltpu.VMEM((tm, D_MODEL), BF16)],
        compiler_params=_params(("parallel", "arbitrary")),
        name="norm_proj",
    )(x, g, w)


def _ffn_kernel(x_ref, gpre_ref, gpost_ref, win_ref, wd_ref, o_ref):
    x = x_ref[...]
    h = _rms(x, gpre_ref[...]).astype(BF16)
    acc = None
    for c0 in range(0, D_FF, FFN_TF):
        gt = _dot(h, win_ref[:, c0:c0 + FFN_TF])
        up = _dot(h, win_ref[:, D_FF + c0:D_FF + c0 + FFN_TF])
        act = (gt * jax.nn.sigmoid(gt) * up).astype(BF16)
        part = _dot(act, wd_ref[c0:c0 + FFN_TF, :])
        acc = part if acc is None else acc + part
    o_ref[...] = x + _rms(acc, gpost_ref[...])


def _ffn(x, gpre, gpost, w_in, w_down, layer):
    t = x.shape[0]
    tm = min(FFN_TM, t)
    once = pl.Buffered(1)
    return pl.pallas_call(
        _ffn_kernel,
        grid=(t // tm,),
        in_specs=[
            pl.BlockSpec((tm, D_MODEL), lambda i: (i, 0)),
            pl.BlockSpec((None, 1, D_MODEL), lambda i: (layer, 0, 0)),
            pl.BlockSpec((None, 1, D_MODEL), lambda i: (layer, 0, 0)),
            pl.BlockSpec((None, D_MODEL, 2 * D_FF), lambda i: (layer, 0, 0), pipeline_mode=once),
            pl.BlockSpec((None, D_FF, D_MODEL), lambda i: (layer, 0, 0), pipeline_mode=once),
        ],
        out_specs=pl.BlockSpec((tm, D_MODEL), lambda i: (i, 0)),
        out_shape=jax.ShapeDtypeStruct((t, D_MODEL), F32),
        compiler_params=_params(("parallel",)),
        name="ffn",
    )(x, gpre, gpost, w_in, w_down)


def _attn_prompt_kernel(q0, q1, q2, k0, k1, k2, v0, v1, v2, kh0, kh1, kh2, vh0, vh1, vh2,
                        o_ref, qd, kd, vd, m_s, l_s, a_s, bias_s, split_s):
    q_refs, k_refs, v_refs = (q0, q1, q2), (k0, k1, k2), (v0, v1, v2)
    kh_refs, vh_refs = (kh0, kh1, kh2), (vh0, vh1, vh2)
    first_tile = pl.program_id(2) == 0
    p_rows = q0.shape[0]
    nb = N_BACK

    qi = lax.broadcasted_iota(jnp.int32, (nb, 2 * nb), 0)
    mi = lax.broadcasted_iota(jnp.int32, (nb, 2 * nb), 1)
    band = (mi >= qi) & (mi <= qi + nb)
    bias_s[0] = jnp.where(band, 0.0, NEG_INF)
    bias_s[1] = jnp.where(band & (mi >= nb), 0.0, NEG_INF)

    def residues(ref, d, slot):
        rows = ref.shape[0]
        if d == 1:
            return lambda r: ref[...]
        if d <= SPLIT_STRIDE:
            return lambda r: ref[pl.ds(r, rows // d, stride=d), :]
        assert d % SPLIT_STRIDE == 0 and d // SPLIT_STRIDE <= SPLIT_STRIDE
        m = rows // SPLIT_STRIDE
        for r4 in range(SPLIT_STRIDE):
            split_s[slot, r4 * m:(r4 + 1) * m, :] = ref[pl.ds(r4, m, stride=SPLIT_STRIDE), :]
        d2 = d // SPLIT_STRIDE
        return lambda r: split_s[slot, pl.ds((r % SPLIT_STRIDE) * m + r // SPLIT_STRIDE, m // d2, stride=d2), :]

    for g, d in enumerate(DILATIONS):
        n = p_rows // d
        stride = None if d == 1 else d
        q_res, k_res, v_res = residues(q_refs[g], d, 0), residues(k_refs[g], d, 1), residues(v_refs[g], d, 2)
        kh_res, vh_res = residues(kh_refs[g], d, 3), residues(vh_refs[g], d, 4)
        for r in range(d):
            qd[g, r * n:(r + 1) * n, :] = q_res(r).astype(BF16)
            base = r * (n + nb)
            kd[g, base:base + nb, :] = kh_res(r).astype(BF16)
            kd[g, base + nb:base + nb + n, :] = k_res(r).astype(BF16)
            vd[g, base:base + nb, :] = vh_res(r).astype(BF16)
            vd[g, base + nb:base + nb + n, :] = v_res(r).astype(BF16)

        for r in range(d):
            for s in range(n // nb):
                qoff = r * n + s * nb
                koff = r * (n + nb) + s * nb
                q = qd[g, qoff:qoff + nb, :]
                kk = kd[g, koff:koff + 2 * nb, :]
                vv = vd[g, koff:koff + 2 * nb, :]
                bias = bias_s[first_tile.astype(jnp.int32)] if s == 0 else bias_s[0]
                sc = _dot_nt(q, kk) * SCALE + bias
                m = jnp.max(sc, axis=-1, keepdims=True)
                p = jnp.exp(sc - m)
                l = jnp.sum(p, axis=-1, keepdims=True)
                acc = _dot(p.astype(BF16), vv)
                idx = pl.ds(r + d * nb * s, nb, stride=stride)
                m_s[g, idx, :] = jnp.broadcast_to(m, (nb, HEAD_DIM))
                l_s[g, idx, :] = jnp.broadcast_to(l, (nb, HEAD_DIM))
                a_s[g, idx, :] = acc

    for c in range(p_rows // nb):
        rows = slice(c * nb, (c + 1) * nb)
        ms = [m_s[g, rows, :] for g in range(N_GROUPS)]
        mm = jnp.maximum(jnp.maximum(ms[0], ms[1]), ms[2])
        num = jnp.zeros((nb, HEAD_DIM), F32)
        den = jnp.zeros((nb, HEAD_DIM), F32)
        for g in range(N_GROUPS):
            w = jnp.exp(ms[g] - mm)
            num = num + w * a_s[g, rows, :]
            den = den + w * l_s[g, rows, :]
        o_ref[rows, :] = (num / den).astype(o_ref.dtype)


def _attn_prompt(qkv, batch, seq):
    t = qkv.shape[1]
    p = ATT_P
    tiles = seq // p
    nh = ATT_W // HEAD_DIM

    def main_spec(sec, g):
        return pl.BlockSpec((None, p, HEAD_DIM),
                            lambda b, h, i: (sec * nh + g * HEADS + h, b * tiles + i, 0))

    def halo_spec(sec, g):
        rows = N_BACK * DILATIONS[g]
        per_tile = p // rows
        per_seq = seq // rows
        return pl.BlockSpec(
            (None, rows, HEAD_DIM),
            lambda b, h, i: (sec * nh + g * HEADS + h, jnp.maximum(b * per_seq + i * per_tile - 1, 0), 0))

    in_specs = ([main_spec(0, g) for g in range(N_GROUPS)]
                + [main_spec(1, g) for g in range(N_GROUPS)]
                + [main_spec(2, g) for g in range(N_GROUPS)]
                + [halo_spec(1, g) for g in range(N_GROUPS)]
                + [halo_spec(2, g) for g in range(N_GROUPS)])
    kv_rows = p + N_BACK * max(DILATIONS)
    return pl.pallas_call(
        _attn_prompt_kernel,
        grid=(batch, HEADS, tiles),
        in_specs=in_specs,
        out_specs=pl.BlockSpec((p, HEAD_DIM), lambda b, h, i: (b * tiles + i, h)),
        out_shape=jax.ShapeDtypeStruct((t, GROUP_W), BF16),
        scratch_shapes=[
            pltpu.VMEM((N_GROUPS, p, HEAD_DIM), BF16),
            pltpu.VMEM((N_GROUPS, kv_rows, HEAD_DIM), BF16),
            pltpu.VMEM((N_GROUPS, kv_rows, HEAD_DIM), BF16),
            pltpu.VMEM((N_GROUPS, p, HEAD_DIM), F32),
            pltpu.VMEM((N_GROUPS, p, HEAD_DIM), F32),
            pltpu.VMEM((N_GROUPS, p, HEAD_DIM), F32),
            pltpu.VMEM((2, N_BACK, 2 * N_BACK), F32),
            pltpu.VMEM((5, max(p, N_BACK * max(DILATIONS)), HEAD_DIM), F32),
        ],
        compiler_params=_params(("parallel", "parallel", "arbitrary")),
        name="attn_prompt",
    )(*([qkv] * 15))


KV_ROWS = 2 * HEADS
KV_TILE = 256


def _attn_sample_kernel(qkv_ref, c0_ref, c1_ref, c2_ref, o_ref, bias0, bias1, bias2, biasn):
    bb = c0_ref.shape[0]
    t_new = qkv_ref.shape[1] // bb
    nrow = HEADS * t_new
    d1 = DILATIONS[1]

    @pl.when(pl.program_id(0) == 0)
    def _():
        def row_ids(ncol):
            shape = (HEADS, t_new, ncol)
            h = lax.broadcasted_iota(jnp.int32, shape, 0).reshape(nrow, ncol)
            t = lax.broadcasted_iota(jnp.int32, shape, 1).reshape(nrow, ncol)
            return h, t

        def cache_ids(ncol):
            col = lax.broadcasted_iota(jnp.int32, (nrow, ncol), 1)
            return _div_pow2(col, KV_ROWS), _mod_pow2(col, KV_ROWS)

        h, t = row_ids(bias0.shape[1])
        c, j = cache_ids(bias0.shape[1])
        bias0[...] = jnp.where((j == h) & (c >= t), 0.0, NEG_INF)
        h, t = row_ids(bias1.shape[1])
        c, j = cache_ids(bias1.shape[1])
        bias1[...] = jnp.where((j == h) & (c >= t) & (_mod_pow2(c - t, d1) == 0), 0.0, NEG_INF)
        h, t = row_ids(bias2.shape[2])
        _, j = cache_ids(bias2.shape[2])
        for r in range(t_new):
            bias2[r] = jnp.where((j == h) & (t == r), 0.0, NEG_INF)
        h, t = row_ids(nrow)
        col = lax.broadcasted_iota(jnp.int32, (nrow, nrow), 1)
        h2, t2 = _div_pow2(col, t_new), _mod_pow2(col, t_new)
        for g, d in enumerate(DILATIONS):
            biasn[g] = jnp.where((h2 == h) & (t2 <= t) & (_mod_pow2(t - t2, d) == 0), 0.0, NEG_INF)

    for b in range(bb):
        rows = slice(b * t_new, (b + 1) * t_new)

        def head_major(sec, g):
            hc0 = (sec * N_GROUPS + g) * HEADS
            return jnp.concatenate([qkv_ref[hc0 + h, rows, :] for h in range(HEADS)], axis=0).astype(BF16)

        blocks = []
        for g in range(N_GROUPS):
            q = head_major(0, g)
            if g == 0:
                cached = [(c0_ref[b], bias0[...])]
            elif g == 1:
                cached = [(c1_ref[b], bias1[...])]
            else:
                cached = [(c2_ref[b, :, r * KV_ROWS:(r + 1) * KV_ROWS, :].reshape(-1, HEAD_DIM), bias2[r])
                          for r in range(t_new)]
            for x, bias in cached:
                xb = x.astype(BF16)
                blocks.append((_dot_nt(q, xb) * SCALE + bias, xb, True))
            kn = head_major(1, g)
            vn = head_major(2, g)
            blocks.append((_dot_nt(q, kn) * SCALE + biasn[g], vn, False))

        m = None
        for sc, _, _ in blocks:
            mb = jnp.max(sc, axis=-1, keepdims=True)
            m = mb if m is None else jnp.maximum(m, mb)
        l = jnp.zeros((nrow, 1), F32)
        acc = jnp.zeros((nrow, HEAD_DIM), F32)
        for sc, val, rotate in blocks:
            p = jnp.exp(sc - m)
            l = l + jnp.sum(p, axis=-1, keepdims=True)
            if rotate:
                p = pltpu.roll(p, HEADS, axis=1)
            acc = acc + _dot(p.astype(BF16), val)
        out = acc / l
        for h in range(HEADS):
            o_ref[rows, h * HEAD_DIM:(h + 1) * HEAD_DIM] = out[h * t_new:(h + 1) * t_new]


def _attn_sample(qkv, c0, c1, c2, layer, t_new):
    t = qkv.shape[1]
    nreq = t // t_new
    bb = SAMPLE_ATT_BB
    nrow = HEADS * t_new
    res_rows = t_new * KV_ROWS
    return pl.pallas_call(
        _attn_sample_kernel,
        grid=(nreq // bb,),
        in_specs=[
            pl.BlockSpec((qkv.shape[0], bb * t_new, HEAD_DIM), lambda i: (0, i, 0)),
            pl.BlockSpec((None, bb, c0.shape[2], HEAD_DIM), lambda i: (layer, i, 0, 0)),
            pl.BlockSpec((None, bb, c1.shape[2], HEAD_DIM), lambda i: (layer, i, 0, 0)),
            pl.BlockSpec((None, bb, c2.shape[2], res_rows, HEAD_DIM), lambda i: (layer, i, 0, 0, 0)),
        ],
        out_specs=pl.BlockSpec((bb * t_new, GROUP_W), lambda i: (i, 0)),
        out_shape=jax.ShapeDtypeStruct((t, GROUP_W), F32),
        scratch_shapes=[
            pltpu.VMEM((nrow, c0.shape[2]), F32),
            pltpu.VMEM((nrow, c1.shape[2]), F32),
            pltpu.VMEM((t_new, nrow, c2.shape[2] * KV_ROWS), F32),
            pltpu.VMEM((N_GROUPS, nrow, nrow), F32),
        ],
        compiler_params=_params(("arbitrary",)),
        name="attn_sample",
    )(qkv, c0, c1, c2)


def _mix_tail(x, att, pooled, y, gates, w):
    (watt, wbd, pscale, wpo, cb, lng, lnb, wco, wout, gpost) = w
    br_a = _dot(att.astype(BF16), watt[...])
    z = _dot(pooled.astype(BF16), wbd[...]) * pscale[...]
    br_b = _dot(z.astype(BF16), wpo[...])
    yf = y + cb[...]
    mu = jnp.mean(yf, axis=-1, keepdims=True)
    yc = yf - mu
    var = jnp.mean(yc * yc, axis=-1, keepdims=True)
    yn = yc * lax.rsqrt(var + EPS) * lng[...] + lnb[...]
    conv_out = yn * jax.nn.sigmoid(yn)
    br_c = _dot(conv_out.astype(BF16), wco[...])
    gs = gates.astype(F32)
    mixed = (gs[:, :D_MODEL] * br_a + gs[:, D_MODEL:2 * D_MODEL] * br_b
             + gs[:, 2 * D_MODEL:] * br_c)
    m = _dot(mixed.astype(BF16), wout[...])
    return x + _rms(m, gpost[...])


def _mixer_prompt_kernel(x_ref, att_ref, up_ref, uph_ref, ug_ref, ugh_ref, gates_ref,
                         watt, wbd, pscale, wpo, cw, cb, lng, lnb, wco, wout, gpost,
                         o_ref, nconv_ref, pbuf, cbuf, cshift):
    i = pl.program_id(1)
    p_rows = x_ref.shape[0]
    keep = (i > 0).astype(F32)

    u = up_ref[...]
    pbuf[0:POOL_HALO, :] = uph_ref[...] * keep
    pbuf[POOL_HALO:POOL_HALO + p_rows, :] = u
    pos = i * p_rows + lax.broadcasted_iota(jnp.int32, (p_rows, 1), 0)
    parts = []
    for gi, w in enumerate(POOL_WINDOWS):
        cs = slice(gi * POOL_GC, (gi + 1) * POOL_GC)
        acc = u[:, cs]
        for j in range(1, w):
            acc = acc + pbuf[POOL_HALO - j:POOL_HALO - j + p_rows, cs]
        cnt = jnp.minimum(w, pos + 1).astype(F32)
        parts.append(acc / cnt - u[:, cs])
    pooled = jnp.concatenate(parts, axis=1)

    cbuf[0:CONV_HALO, :] = ugh_ref[...] * keep
    cbuf[CONV_HALO:CONV_HALO + p_rows, :] = ug_ref[...]
    span = p_rows + CONV_HALO - SUBLANES
    for s in range(1, SUBLANES):
        cshift[s - 1, 0:span, :] = cbuf[s:s + span, :]
    off0 = CONV_HALO - CONV_STATE
    chunks = []
    for r0 in range(0, p_rows, CONV_CHUNK):
        yc = None
        for j in range(CONV_K):
            base, s = (off0 + j) // SUBLANES * SUBLANES + r0, (off0 + j) % SUBLANES
            win = (cbuf[base:base + CONV_CHUNK, :] if s == 0
                   else cshift[s - 1, base:base + CONV_CHUNK, :])
            term = cw[j:j + 1, :] * win
            yc = term if yc is None else yc + term
        chunks.append(yc)
    y = jnp.concatenate(chunks, axis=0)

    @pl.when(i == pl.num_programs(1) - 1)
    def _():
        nconv_ref[...] = cbuf[p_rows:p_rows + CONV_HALO, :]

    o_ref[...] = _mix_tail(x_ref[...], att_ref[...], pooled, y, gates_ref[...],
                           (watt, wbd, pscale, wpo, cb, lng, lnb, wco, wout, gpost))


def _weight_specs(layer):
    def vec(width):
        return pl.BlockSpec((None, 1, width), lambda *_: (layer, 0, 0))

    def mat(rows, cols):
        return pl.BlockSpec((None, rows, cols), lambda *_: (layer, 0, 0))

    return vec, mat


def _mixer_prompt(x, att, upc, gates, wts, layer, batch, seq):
    t = x.shape[0]
    p = MIX_P
    tiles = seq // p
    vec, mat = _weight_specs(layer)

    def row(i_b, i_t):
        return i_b * tiles + i_t

    def halo(rows, colblk):
        per_tile = p // rows
        per_seq = seq // rows
        return pl.BlockSpec(
            (rows, POOL_W), lambda b, i: (jnp.maximum(b * per_seq + i * per_tile - 1, 0), colblk))

    in_specs = [
        pl.BlockSpec((p, D_MODEL), lambda b, i: (row(b, i), 0)),
        pl.BlockSpec((p, GROUP_W), lambda b, i: (row(b, i), 0)),
        pl.BlockSpec((p, POOL_W), lambda b, i: (row(b, i), 0)),
        halo(POOL_HALO, 0),
        pl.BlockSpec((p, CONV_W), lambda b, i: (row(b, i), 1)),
        halo(CONV_HALO, 1),
        pl.BlockSpec((p, GATE_W), lambda b, i: (row(b, i), 0)),
        mat(GROUP_W, D_MODEL), mat(POOL_W, POOL_W), vec(POOL_W), mat(POOL_W, D_MODEL),
        mat(CONV_K, CONV_W), vec(CONV_W), vec(CONV_W), vec(CONV_W), mat(CONV_W, D_MODEL),
        mat(D_MODEL, D_MODEL), vec(D_MODEL),
    ]
    return pl.pallas_call(
        _mixer_prompt_kernel,
        grid=(batch, tiles),
        in_specs=in_specs,
        out_specs=[
            pl.BlockSpec((p, D_MODEL), lambda b, i: (row(b, i), 0)),
            pl.BlockSpec((None, CONV_HALO, CONV_W), lambda b, i: (b, 0, 0)),
        ],
        out_shape=[
            jax.ShapeDtypeStruct((t, D_MODEL), F32),
            jax.ShapeDtypeStruct((batch, CONV_HALO, CONV_W), F32),
        ],
        scratch_shapes=[
            pltpu.VMEM((POOL_HALO + p, POOL_W), F32),
            pltpu.VMEM((CONV_HALO + p, CONV_W), F32),
            pltpu.VMEM((SUBLANES - 1, CONV_HALO + p - SUBLANES, CONV_W), F32),
        ],
        compiler_params=_params(("parallel", "arbitrary")),
        name="mixer_prompt",
    )(x, att, upc, upc, upc, upc, gates, *wts)


def _mixer_sample_kernel(x_ref, att_ref, up_ref, ug_ref, gates_ref, sp_ref, sc_ref,
                         watt, wbd, pscale, wpo, cw, cb, lng, lnb, wco, wout, gpost,
                         o_ref, npool_ref, nconv_ref):
    t_new = x_ref.shape[0]
    bb = x_ref.shape[1]

    def rows(ref):
        return jnp.concatenate([ref[t] for t in range(t_new)], axis=0)

    new_p = [up_ref[t] for t in range(t_new)]
    hist_p = [sp_ref[j] for j in range(POOL_STATE)] + new_p
    pooled = []
    for t in range(t_new):
        parts = []
        for gi, w in enumerate(POOL_WINDOWS):
            cs = slice(gi * POOL_GC, (gi + 1) * POOL_GC)
            acc = new_p[t][:, cs]
            for j in range(1, w):
                acc = acc + hist_p[POOL_STATE + t - j][:, cs]
            cnt = float(min(w, PAST_LEN + t + 1))
            parts.append(acc / cnt - new_p[t][:, cs])
        pooled.append(jnp.concatenate(parts, axis=1))
    for j in range(POOL_STATE):
        npool_ref[j] = hist_p[t_new + j]

    hist_c = [sc_ref[j] for j in range(CONV_STATE)] + [ug_ref[t] for t in range(t_new)]
    ys = []
    for t in range(t_new):
        y = cw[0:1, :] * hist_c[t]
        for j in range(1, CONV_K):
            y = y + cw[j:j + 1, :] * hist_c[t + j]
        ys.append(y)
    for j in range(CONV_STATE):
        nconv_ref[j] = hist_c[t_new + j]

    out = _mix_tail(rows(x_ref), rows(att_ref), jnp.concatenate(pooled, axis=0),
                    jnp.concatenate(ys, axis=0), rows(gates_ref),
                    (watt, wbd, pscale, wpo, cb, lng, lnb, wco, wout, gpost))
    for t in range(t_new):
        o_ref[t] = out[t * bb:(t + 1) * bb]


def _mixer_sample(x, att, upc, gates, sp, sc, wts, layer):
    t_new, nreq, _ = x.shape
    bb = SAMPLE_MIX_BB
    vec, mat = _weight_specs(layer)

    def act(width, colblk=0):
        return pl.BlockSpec((t_new, bb, width), lambda i: (0, i, colblk))

    in_specs = [
        act(D_MODEL), act(GROUP_W), act(POOL_W, 0), act(CONV_W, 1), act(GATE_W),
        pl.BlockSpec((None, POOL_STATE, bb, POOL_W), lambda i: (layer, 0, i, 0)),
        pl.BlockSpec((None, CONV_STATE, bb, CONV_W), lambda i: (layer, 0, i, 0)),
        mat(GROUP_W, D_MODEL), mat(POOL_W, POOL_W), vec(POOL_W), mat(POOL_W, D_MODEL),
        mat(CONV_K, CONV_W), vec(CONV_W), vec(CONV_W), vec(CONV_W), mat(CONV_W, D_MODEL),
        mat(D_MODEL, D_MODEL), vec(D_MODEL),
    ]
    return pl.pallas_call(
        _mixer_sample_kernel,
        grid=(nreq // bb,),
        in_specs=in_specs,
        out_specs=[
            act(D_MODEL),
            pl.BlockSpec((POOL_STATE, bb, POOL_W), lambda i: (0, i, 0)),
            pl.BlockSpec((CONV_STATE, bb, CONV_W), lambda i: (0, i, 0)),
        ],
        out_shape=[
            jax.ShapeDtypeStruct((t_new, nreq, D_MODEL), F32),
            jax.ShapeDtypeStruct((POOL_STATE, nreq, POOL_W), F32),
            jax.ShapeDtypeStruct((CONV_STATE, nreq, CONV_W), F32),
        ],
        compiler_params=_params(("parallel",)),
        name="mixer_sample",
    )(x, att, upc, upc, gates, sp, sc, *wts)


def _kv_pack_kernel(*refs, depth):
    ins, (o0, o1, o2) = refs[:6 * depth], refs[6 * depth:]
    layer, t, nt = pl.program_id(0), pl.program_id(2), pl.num_programs(2)
    t1 = o1.shape[0] // (KV_TILE * KV_ROWS)

    def pack(o_ref, k_ref, v_ref, row0=0):
        n = k_ref.shape[1]
        for j in range(HEADS):
            o_ref[pl.ds(row0 + j, n, stride=KV_ROWS), :] = k_ref[j]
            o_ref[pl.ds(row0 + HEADS + j, n, stride=KV_ROWS), :] = v_ref[j]

    for lyr in range(depth):
        k2, v2, k1, v1, k0, v0 = ins[6 * lyr:6 * lyr + 6]

        @pl.when(layer == lyr)
        def _(k2=k2, v2=v2, k1=k1, v1=v1, k0=k0, v0=v0):
            pack(o2, k2, v2)
            for s in range(t1):
                @pl.when(t == nt - t1 + s)
                def _(s=s):
                    pack(o1, k1, v1, s * KV_TILE * KV_ROWS)

            @pl.when(t == nt - 1)
            def _():
                pack(o0, k0, v0)


def _kv_pack(qkvs, batch, seq):
    depth = len(qkvs)
    w0, w1, w2 = (min(w, seq) for w in WINDOWS)
    assert w2 % KV_TILE == 0 and w1 % KV_TILE == 0 and seq % KV_TILE == 0 and seq % w0 == 0
    assert w0 <= KV_TILE and w2 >= w1
    nt, t1 = w2 // KV_TILE, w1 // KV_TILE
    per_seq = seq // KV_TILE

    def frozen(lyr, fn):
        def index_map(l, b, t):
            b = jnp.where(l == lyr, b, jnp.where(l < lyr, 0, batch - 1))
            t = jnp.where(l == lyr, t, jnp.where(l < lyr, 0, nt - 1))
            return fn(b, t)
        return index_map

    def col(sec, g):
        return sec * N_GROUPS + g

    in_specs, args = [], []
    for lyr in range(depth):
        for g, sec in ((2, 1), (2, 2), (1, 1), (1, 2), (0, 1), (0, 2)):
            if g == 2:
                spec = pl.BlockSpec((HEADS, KV_TILE, HEAD_DIM), frozen(
                    lyr, lambda b, t, c=col(sec, g): (c, b * per_seq + per_seq - nt + t, 0)))
            elif g == 1:
                spec = pl.BlockSpec((HEADS, KV_TILE, HEAD_DIM), frozen(
                    lyr, lambda b, t, c=col(sec, g):
                    (c, b * per_seq + per_seq - t1 + jnp.clip(t - (nt - t1), 0, t1 - 1), 0)))
            else:
                spec = pl.BlockSpec((HEADS, w0, HEAD_DIM), frozen(
                    lyr, lambda b, t, c=col(sec, g): (c, (b + 1) * (seq // w0) - 1, 0)))
            in_specs.append(spec)
            args.append(qkvs[lyr])

    def out(w):
        return (jax.ShapeDtypeStruct((depth, batch, w * KV_ROWS, HEAD_DIM), F32))

    o0, o1, o2 = pl.pallas_call(
        partial(_kv_pack_kernel, depth=depth),
        grid=(depth, batch, nt),
        in_specs=in_specs,
        out_specs=[
            pl.BlockSpec((None, None, w0 * KV_ROWS, HEAD_DIM), lambda l, b, t: (l, b, 0, 0)),
            pl.BlockSpec((None, None, w1 * KV_ROWS, HEAD_DIM), lambda l, b, t: (l, b, 0, 0)),
            pl.BlockSpec((None, None, KV_TILE * KV_ROWS, HEAD_DIM), lambda l, b, t: (l, b, t, 0)),
        ],
        out_shape=[out(w0), out(w1), out(w2)],
        compiler_params=_params(("arbitrary", "arbitrary", "arbitrary")),
        name="kv_pack",
    )(*args)
    return tuple(o.reshape(depth, batch, w, 2, HEADS, HEAD_DIM) for o, w in ((o0, w0), (o1, w1), (o2, w2)))


def _kv_rows(qkv, g, nreq, t_new):
    nh = N_GROUPS * HEADS
    kv = jnp.stack([qkv[sec * nh + g * HEADS:sec * nh + (g + 1) * HEADS] for sec in (1, 2)])
    return jnp.moveaxis(kv, 2, 0).reshape(nreq, t_new, 2, HEADS, HEAD_DIM)


def kernel(x_prompt, x_sample, cache_kv_w128, cache_kv_w512, cache_kv_w2048, state_pool, state_conv,
           w_in, w_att_o, w_pool_g, pool_scale, w_pool_o, conv_w, conv_b, ln_g, ln_b, w_conv_o, w_out,
           g_mix_pre, g_mix_post, w_ffn_in, w_ffn_down, g_ffn_pre, g_ffn_post):
    batch, seq, _ = x_prompt.shape
    nreq, t_new, _ = x_sample.shape
    depth = w_in.shape[0]

    w_in_b = w_in.astype(BF16)
    w_ffn_in_b = w_ffn_in.astype(BF16)
    w_ffn_down_b = w_ffn_down.astype(BF16)
    eye = jnp.eye(len(POOL_WINDOWS), dtype=F32)
    w_bd = (w_pool_g[:, :, :, None, :] * eye[None, :, None, :, None]).reshape(depth, POOL_W, POOL_W)

    def vec(a):
        return a.reshape(depth, 1, a.shape[-1])

    mix_w = (w_att_o.astype(BF16), w_bd.astype(BF16), vec(pool_scale), w_pool_o.astype(BF16),
             conv_w, vec(conv_b), vec(ln_g), vec(ln_b), w_conv_o.astype(BF16), w_out.astype(BF16),
             vec(g_mix_post))
    g_pre, g_fpre, g_fpost = vec(g_mix_pre), vec(g_ffn_pre), vec(g_ffn_post)

    c0 = cache_kv_w128.reshape(depth, nreq, cache_kv_w128.shape[2] * KV_ROWS, HEAD_DIM)
    c1 = cache_kv_w512.reshape(depth, nreq, cache_kv_w512.shape[2] * KV_ROWS, HEAD_DIM)
    d2 = DILATIONS[2]
    c2 = cache_kv_w2048.reshape(depth, nreq, cache_kv_w2048.shape[2] // d2, d2 * KV_ROWS, HEAD_DIM)
    sp_t = jnp.swapaxes(state_pool, 1, 2)
    sc_t = jnp.swapaxes(state_conv, 1, 2)

    xp = x_prompt.reshape(batch * seq, D_MODEL)
    xs = jnp.swapaxes(x_sample, 0, 1).reshape(t_new * nreq, D_MODEL)
    outs = {k: [] for k in ("ppool", "pconv", "skv0", "skv1", "skv2", "spool", "sconv")}
    qkvs = []

    for l in range(depth):
        qkv, upc, gates = _norm_proj(xp, g_pre, w_in_b, l)
        att = _attn_prompt(qkv, batch, seq)
        x1, nconv = _mixer_prompt(xp, att, upc, gates, mix_w, l, batch, seq)
        xp = _ffn(x1, g_fpre, g_fpost, w_ffn_in_b, w_ffn_down_b, l)
        qkvs.append(qkv)
        outs["ppool"].append(upc.reshape(batch, seq, UPC_W)[:, seq - POOL_STATE:, :POOL_W])
        outs["pconv"].append(nconv[:, CONV_HALO - CONV_STATE:])

        qkv_s, upc_s, gates_s = _norm_proj(xs, g_pre, w_in_b, l)
        qkv_rm = jnp.swapaxes(qkv_s.reshape(-1, t_new, nreq, HEAD_DIM), 1, 2)
        qkv_rm = qkv_rm.reshape(-1, nreq * t_new, HEAD_DIM)
        att_s = _attn_sample(qkv_rm, c0, c1, c2, l, t_new)
        att_s = jnp.swapaxes(att_s.reshape(nreq, t_new, GROUP_W), 0, 1)
        x1s, npool_s, nconv_s = _mixer_sample(
            xs.reshape(t_new, nreq, D_MODEL), att_s, upc_s.reshape(t_new, nreq, UPC_W),
            gates_s.reshape(t_new, nreq, GATE_W), sp_t, sc_t, mix_w, l)
        xs = _ffn(x1s.reshape(t_new * nreq, D_MODEL), g_fpre, g_fpost, w_ffn_in_b, w_ffn_down_b, l)
        for g in range(N_GROUPS):
            outs[f"skv{g}"].append(_kv_rows(qkv_rm, g, nreq, t_new))
        outs["spool"].append(jnp.swapaxes(npool_s, 0, 1))
        outs["sconv"].append(jnp.swapaxes(nconv_s, 0, 1))

    st = {k: jnp.stack(v) for k, v in outs.items()}
    pkv0, pkv1, pkv2 = _kv_pack(qkvs, batch, seq)
    return (xp.reshape(batch, seq, D_MODEL), jnp.swapaxes(xs.reshape(t_new, nreq, D_MODEL), 0, 1),
            pkv0, pkv1, pkv2, st["ppool"], st["pconv"],
            st["skv0"], st["skv1"], st["skv2"], st["spool"], st["sconv"])
```

```python
from functools import partial

import jax
import jax.numpy as jnp
from jax import lax
from jax.experimental import pallas as pl
from jax.experimental.pallas import tpu as pltpu

F32 = jnp.float32
BF16 = jnp.bfloat16

D_MODEL = 1024
DEPTH = 4
PAST_LEN = 2048
WINDOWS = (128, 512, 2048)
DILATIONS = (1, 4, 16)
N_GROUPS = 3
HEADS = 4
HEAD_DIM = 128
GROUP_W = HEADS * HEAD_DIM
ATT_W = N_GROUPS * GROUP_W
N_BACK = 128
POOL_WINDOWS = (2, 4, 8, 16)
POOL_GC = 128
POOL_W = 512
POOL_STATE = 15
CONV_W = 512
CONV_K = 31
CONV_STATE = 30
D_FF = 2816
EPS = 1e-6
NEG_INF = -1e30
SCALE = HEAD_DIM ** -0.5

QKV_W = 3 * ATT_W
UPG_W = POOL_W + 2 * CONV_W
UPC_W = POOL_W + CONV_W
GATE_W = 3 * D_MODEL

VMEM_LIMIT = 56 * 1024 * 1024

PROJ_TM = 1024
PROJ_TN = 1536
FFN_TM = 512
FFN_TF = 1408
ATT_P = 2048
MIX_P = 512
POOL_HALO = 16
CONV_HALO = 32
CONV_CHUNK = 32
SPLIT_STRIDE = 4
SAMPLE_ATT_BB = 2
SAMPLE_MIX_BB = 32


def _params(sem):
    return pltpu.CompilerParams(dimension_semantics=sem, vmem_limit_bytes=VMEM_LIMIT)


def _rms(x, g):
    ms = jnp.mean(x * x, axis=-1, keepdims=True)
    return x * lax.rsqrt(ms + EPS) * g


def _dot(a, b):
    return jnp.dot(a, b, preferred_element_type=F32)


def _dot_nt(a, b):
    return lax.dot_general(a, b, (((1,), (1,)), ((), ())), preferred_element_type=F32)


def _div_pow2(x, n):
    assert n & (n - 1) == 0
    return lax.shift_right_arithmetic(x, n.bit_length() - 1)


def _mod_pow2(x, n):
    assert n & (n - 1) == 0
    return x & (n - 1)


QKV_BLKS = QKV_W // PROJ_TN
UPG_BLKS = UPG_W // PROJ_TN
GATE_BLKS = GATE_W // PROJ_TN


def _norm_proj_kernel(x_ref, g_ref, w_ref, qkv_ref, upc_ref, gates_ref, h_scr):
    j = pl.program_id(1)

    @pl.when(j == 0)
    def _():
        h_scr[...] = _rms(x_ref[...], g_ref[...]).astype(BF16)

    @pl.when(j < QKV_BLKS)
    def _():
        r = _dot(h_scr[...], w_ref[...])
        for hc in range(qkv_ref.shape[0]):
            qkv_ref[hc] = r[:, hc * HEAD_DIM:(hc + 1) * HEAD_DIM]

    @pl.when((j >= QKV_BLKS) & (j < QKV_BLKS + UPG_BLKS))
    def _():
        r = _dot(h_scr[...], w_ref[...])
        upc_ref[:, :POOL_W] = r[:, :POOL_W]
        upc_ref[:, POOL_W:] = r[:, POOL_W:POOL_W + CONV_W] * jax.nn.sigmoid(r[:, POOL_W + CONV_W:])

    @pl.when(j >= QKV_BLKS + UPG_BLKS)
    def _():
        gates_ref[...] = jax.nn.sigmoid(_dot(h_scr[...], w_ref[...])).astype(gates_ref.dtype)


def _norm_proj(x, g, w, layer):
    t = x.shape[0]
    tm = min(PROJ_TM, t)
    nj = QKV_BLKS + UPG_BLKS + GATE_BLKS
    assert UPG_BLKS == 1
    lo_g = QKV_BLKS + UPG_BLKS
    return pl.pallas_call(
        _norm_proj_kernel,
        grid=(t // tm, nj),
        in_specs=[
            pl.BlockSpec((tm, D_MODEL), lambda i, j: (i, 0)),
            pl.BlockSpec((None, 1, D_MODEL), lambda i, j: (layer, 0, 0)),
            pl.BlockSpec((None, D_MODEL, PROJ_TN), lambda i, j: (layer, 0, j)),
        ],
        out_specs=[
            pl.BlockSpec((PROJ_TN // HEAD_DIM, tm, HEAD_DIM),
                         lambda i, j: (jnp.minimum(j, QKV_BLKS - 1), i, 0)),
            pl.BlockSpec((tm, UPC_W), lambda i, j: (i, 0)),
            pl.BlockSpec((tm, PROJ_TN), lambda i, j: (i, jnp.clip(j - lo_g, 0, GATE_BLKS - 1))),
        ],
        out_shape=[
            jax.ShapeDtypeStruct((QKV_W // HEAD_DIM, t, HEAD_DIM), F32),
            jax.ShapeDtypeStruct((t, UPC_W), F32),
            jax.ShapeDtypeStruct((t, GATE_W), BF16),
        ],
        scratch_shapes=[pltpu.VMEM((tm, D_MODEL), BF16)],
        compiler_params=_params(("parallel", "arbitrary")),
        name="norm_proj",
    )(x, g, w)


def _ffn_kernel(x_ref, gpre_ref, gpost_ref, win_ref, wd_ref, o_ref):
    x = x_ref[...]
    h = _rms(x, gpre_ref[...]).astype(BF16)
    acc = None
    for c0 in range(0, D_FF, FFN_TF):
        gt = _dot(h, win_ref[:, c0:c0 + FFN_TF])
        up = _dot(h, win_ref[:, D_FF + c0:D_FF + c0 + FFN_TF])
        act = (gt * jax.nn.sigmoid(gt) * up).astype(BF16)
        part = _dot(act, wd_ref[c0:c0 + FFN_TF, :])
        acc = part if acc is None else acc + part
    o_ref[...] = x + _rms(acc, gpost_ref[...])


def _ffn(x, gpre, gpost, w_in, w_down, layer):
    t = x.shape[0]
    tm = min(FFN_TM, t)
    once = pl.Buffered(1)
    return pl.pallas_call(
        _ffn_kernel,
        grid=(t // tm,),
        in_specs=[
            pl.BlockSpec((tm, D_MODEL), lambda i: (i, 0)),
            pl.BlockSpec((None, 1, D_MODEL), lambda i: (layer, 0, 0)),
            pl.BlockSpec((None, 1, D_MODEL), lambda i: (layer, 0, 0)),
            pl.BlockSpec((None, D_MODEL, 2 * D_FF), lambda i: (layer, 0, 0), pipeline_mode=once),
            pl.BlockSpec((None, D_FF, D_MODEL), lambda i: (layer, 0, 0), pipeline_mode=once),
        ],
        out_specs=pl.BlockSpec((tm, D_MODEL), lambda i: (i, 0)),
        out_shape=jax.ShapeDtypeStruct((t, D_MODEL), F32),
        compiler_params=_params(("parallel",)),
        name="ffn",
    )(x, gpre, gpost, w_in, w_down)


def _attn_prompt_kernel(q0, q1, q2, k0, k1, k2, v0, v1, v2, kh0, kh1, kh2, vh0, vh1, vh2,
                        o_ref, qd, kd, vd, m_s, l_s, a_s, bias_s, split_s):
    q_refs, k_refs, v_refs = (q0, q1, q2), (k0, k1, k2), (v0, v1, v2)
    kh_refs, vh_refs = (kh0, kh1, kh2), (vh0, vh1, vh2)
    first_tile = pl.program_id(2) == 0
    p_rows = q0.shape[0]
    nb = N_BACK

    qi = lax.broadcasted_iota(jnp.int32, (nb, 2 * nb), 0)
    mi = lax.broadcasted_iota(jnp.int32, (nb, 2 * nb), 1)
    band = (mi >= qi) & (mi <= qi + nb)
    bias_s[0] = jnp.where(band, 0.0, NEG_INF)
    bias_s[1] = jnp.where(band & (mi >= nb), 0.0, NEG_INF)

    def residues(ref, d, slot):
        rows = ref.shape[0]
        if d == 1:
            return lambda r: ref[...]
        if d <= SPLIT_STRIDE:
            return lambda r: ref[pl.ds(r, rows // d, stride=d), :]
        assert d % SPLIT_STRIDE == 0 and d // SPLIT_STRIDE <= SPLIT_STRIDE
        m = rows // SPLIT_STRIDE
        for r4 in range(SPLIT_STRIDE):
            split_s[slot, r4 * m:(r4 + 1) * m, :] = ref[pl.ds(r4, m, stride=SPLIT_STRIDE), :]
        d2 = d // SPLIT_STRIDE
        return lambda r: split_s[slot, pl.ds((r % SPLIT_STRIDE) * m + r // SPLIT_STRIDE, m // d2, stride=d2), :]

    for g, d in enumerate(DILATIONS):
        n = p_rows // d
        stride = None if d == 1 else d
        q_res, k_res, v_res = residues(q_refs[g], d, 0), residues(k_refs[g], d, 1), residues(v_refs[g], d, 2)
        kh_res, vh_res = residues(kh_refs[g], d, 3), residues(vh_refs[g], d, 4)
        for r in range(d):
            qd[g, r * n:(r + 1) * n, :] = q_res(r).astype(BF16)
            base = r * (n + nb)
            kd[g, base:base + nb, :] = kh_res(r).astype(BF16)
            kd[g, base + nb:base + nb + n, :] = k_res(r).astype(BF16)
            vd[g, base:base + nb, :] = vh_res(r).astype(BF16)
            vd[g, base + nb:base + nb + n, :] = v_res(r).astype(BF16)

        for r in range(d):
            for s in range(n // nb):
                qoff = r * n + s * nb
                koff = r * (n + nb) + s * nb
                q = qd[g, qoff:qoff + nb, :]
                kk = kd[g, koff:koff + 2 * nb, :]
                vv = vd[g, koff:koff + 2 * nb, :]
                bias = bias_s[first_tile.astype(jnp.int32)] if s == 0 else bias_s[0]
                sc = _dot_nt(q, kk) * SCALE + bias
                m = jnp.max(sc, axis=-1, keepdims=True)
                p = jnp.exp(sc - m)
                l = jnp.sum(p, axis=-1, keepdims=True)
                acc = _dot(p.astype(BF16), vv)
                idx = pl.ds(r + d * nb * s, nb, stride=stride)
                m_s[g, idx, :] = jnp.broadcast_to(m, (nb, HEAD_DIM))
                l_s[g, idx, :] = jnp.broadcast_to(l, (nb, HEAD_DIM))
                a_s[g, idx, :] = acc

    for c in range(p_rows // nb):
        rows = slice(c * nb, (c + 1) * nb)
        ms = [m_s[g, rows, :] for g in range(N_GROUPS)]
        mm = jnp.maximum(jnp.maximum(ms[0], ms[1]), ms[2])
        num = jnp.zeros((nb, HEAD_DIM), F32)
        den = jnp.zeros((nb, HEAD_DIM), F32)
        for g in range(N_GROUPS):
            w = jnp.exp(ms[g] - mm)
            num = num + w * a_s[g, rows, :]
            den = den + w * l_s[g, rows, :]
        o_ref[rows, :] = (num / den).astype(o_ref.dtype)


def _attn_prompt(qkv, batch, seq):
    t = qkv.shape[1]
    p = ATT_P
    tiles = seq // p
    nh = ATT_W // HEAD_DIM

    def main_spec(sec, g):
        return pl.BlockSpec((None, p, HEAD_DIM),
                            lambda b, h, i: (sec * nh + g * HEADS + h, b * tiles + i, 0))

    def halo_spec(sec, g):
        rows = N_BACK * DILATIONS[g]
        per_tile = p // rows
        per_seq = seq // rows
        return pl.BlockSpec(
            (None, rows, HEAD_DIM),
            lambda b, h, i: (sec * nh + g * HEADS + h, jnp.maximum(b * per_seq + i * per_tile - 1, 0), 0))

    in_specs = ([main_spec(0, g) for g in range(N_GROUPS)]
                + [main_spec(1, g) for g in range(N_GROUPS)]
                + [main_spec(2, g) for g in range(N_GROUPS)]
                + [halo_spec(1, g) for g in range(N_GROUPS)]
                + [halo_spec(2, g) for g in range(N_GROUPS)])
    kv_rows = p + N_BACK * max(DILATIONS)
    return pl.pallas_call(
        _attn_prompt_kernel,
        grid=(batch, HEADS, tiles),
        in_specs=in_specs,
        out_specs=pl.BlockSpec((p, HEAD_DIM), lambda b, h, i: (b * tiles + i, h)),
        out_shape=jax.ShapeDtypeStruct((t, GROUP_W), BF16),
        scratch_shapes=[
            pltpu.VMEM((N_GROUPS, p, HEAD_DIM), BF16),
            pltpu.VMEM((N_GROUPS, kv_rows, HEAD_DIM), BF16),
            pltpu.VMEM((N_GROUPS, kv_rows, HEAD_DIM), BF16),
            pltpu.VMEM((N_GROUPS, p, HEAD_DIM), F32),
            pltpu.VMEM((N_GROUPS, p, HEAD_DIM), F32),
            pltpu.VMEM((N_GROUPS, p, HEAD_DIM), F32),
            pltpu.VMEM((2, N_BACK, 2 * N_BACK), F32),
            pltpu.VMEM((5, max(p, N_BACK * max(DILATIONS)), HEAD_DIM), F32),
        ],
        compiler_params=_params(("parallel", "parallel", "arbitrary")),
        name="attn_prompt",
    )(*([qkv] * 15))


KV_ROWS = 2 * HEADS
KV_TILE = 256


def _attn_sample_kernel(qkv_ref, c0_ref, c1_ref, c2_ref, o_ref, bias0, bias1, bias2, biasn):
    bb = c0_ref.shape[0]
    t_new = qkv_ref.shape[1] // bb
    nrow = HEADS * t_new
    d1 = DILATIONS[1]

    @pl.when(pl.program_id(0) == 0)
    def _():
        def row_ids(ncol):
            shape = (HEADS, t_new, ncol)
            h = lax.broadcasted_iota(jnp.int32, shape, 0).reshape(nrow, ncol)
            t = lax.broadcasted_iota(jnp.int32, shape, 1).reshape(nrow, ncol)
            return h, t

        def cache_ids(ncol):
            col = lax.broadcasted_iota(jnp.int32, (nrow, ncol), 1)
            return _div_pow2(col, KV_ROWS), _mod_pow2(col, KV_ROWS)

        h, t = row_ids(bias0.shape[1])
        c, j = cache_ids(bias0.shape[1])
        bias0[...] = jnp.where((j == h) & (c >= t), 0.0, NEG_INF)
        h, t = row_ids(bias1.shape[1])
        c, j = cache_ids(bias1.shape[1])
        bias1[...] = jnp.where((j == h) & (c >= t) & (_mod_pow2(c - t, d1) == 0), 0.0, NEG_INF)
        h, t = row_ids(bias2.shape[2])
        _, j = cache_ids(bias2.shape[2])
        for r in range(t_new):
            bias2[r] = jnp.where((j == h) & (t == r), 0.0, NEG_INF)
        h, t = row_ids(nrow)
        col = lax.broadcasted_iota(jnp.int32, (nrow, nrow), 1)
        h2, t2 = _div_pow2(col, t_new), _mod_pow2(col, t_new)
        for g, d in enumerate(DILATIONS):
            biasn[g] = jnp.where((h2 == h) & (t2 <= t) & (_mod_pow2(t - t2, d) == 0), 0.0, NEG_INF)

    for b in range(bb):
        rows = slice(b * t_new, (b + 1) * t_new)

        def head_major(sec, g):
            hc0 = (sec * N_GROUPS + g) * HEADS
            return jnp.concatenate([qkv_ref[hc0 + h, rows, :] for h in range(HEADS)], axis=0).astype(BF16)

        blocks = []
        for g in range(N_GROUPS):
            q = head_major(0, g)
            if g == 0:
                cached = [(c0_ref[b], bias0[...])]
            elif g == 1:
                cached = [(c1_ref[b], bias1[...])]
            else:
                cached = [(c2_ref[b, :, r * KV_ROWS:(r + 1) * KV_ROWS, :].reshape(-1, HEAD_DIM), bias2[r])
                          for r in range(t_new)]
            for x, bias in cached:
                xb = x.astype(BF16)
                blocks.append((_dot_nt(q, xb) * SCALE + bias, xb, True))
            kn = head_major(1, g)
            vn = head_major(2, g)
            blocks.append((_dot_nt(q, kn) * SCALE + biasn[g], vn, False))

        m = None
        for sc, _, _ in blocks:
            mb = jnp.max(sc, axis=-1, keepdims=True)
            m = mb if m is None else jnp.maximum(m, mb)
        l = jnp.zeros((nrow, 1), F32)
        acc = jnp.zeros((nrow, HEAD_DIM), F32)
        for sc, val, rotate in blocks:
            p = jnp.exp(sc - m)
            l = l + jnp.sum(p, axis=-1, keepdims=True)
            if rotate:
                p = pltpu.roll(p, HEADS, axis=1)
            acc = acc + _dot(p.astype(BF16), val)
        out = acc / l
        for h in range(HEADS):
            o_ref[rows, h * HEAD_DIM:(h + 1) * HEAD_DIM] = out[h * t_new:(h + 1) * t_new]


def _attn_sample(qkv, c0, c1, c2, layer, t_new):
    t = qkv.shape[1]
    nreq = t // t_new
    bb = SAMPLE_ATT_BB
    nrow = HEADS * t_new
    res_rows = t_new * KV_ROWS
    return pl.pallas_call(
        _attn_sample_kernel,
        grid=(nreq // bb,),
        in_specs=[
            pl.BlockSpec((qkv.shape[0], bb * t_new, HEAD_DIM), lambda i: (0, i, 0)),
            pl.BlockSpec((None, bb, c0.shape[2], HEAD_DIM), lambda i: (layer, i, 0, 0)),
            pl.BlockSpec((None, bb, c1.shape[2], HEAD_DIM), lambda i: (layer, i, 0, 0)),
            pl.BlockSpec((None, bb, c2.shape[2], res_rows, HEAD_DIM), lambda i: (layer, i, 0, 0, 0)),
        ],
        out_specs=pl.BlockSpec((bb * t_new, GROUP_W), lambda i: (i, 0)),
        out_shape=jax.ShapeDtypeStruct((t, GROUP_W), F32),
        scratch_shapes=[
            pltpu.VMEM((nrow, c0.shape[2]), F32),
            pltpu.VMEM((nrow, c1.shape[2]), F32),
            pltpu.VMEM((t_new, nrow, c2.shape[2] * KV_ROWS), F32),
            pltpu.VMEM((N_GROUPS, nrow, nrow), F32),
        ],
        compiler_params=_params(("arbitrary",)),
        name="attn_sample",
    )(qkv, c0, c1, c2)


def _mix_tail(x, att, pooled, y, gates, w):
    (watt, wbd, pscale, wpo, cb, lng, lnb, wco, wout, gpost) = w
    br_a = _dot(att.astype(BF16), watt[...])
    z = _dot(pooled.astype(BF16), wbd[...]) * pscale[...]
    br_b = _dot(z.astype(BF16), wpo[...])
    yf = y + cb[...]
    mu = jnp.mean(yf, axis=-1, keepdims=True)
    yc = yf - mu
    var = jnp.mean(yc * yc, axis=-1, keepdims=True)
    yn = yc * lax.rsqrt(var + EPS) * lng[...] + lnb[...]
    conv_out = yn * jax.nn.sigmoid(yn)
    br_c = _dot(conv_out.astype(BF16), wco[...])
    gs = gates.astype(F32)
    mixed = (gs[:, :D_MODEL] * br_a + gs[:, D_MODEL:2 * D_MODEL] * br_b
             + gs[:, 2 * D_MODEL:] * br_c)
    m = _dot(mixed.astype(BF16), wout[...])
    return x + _rms(m, gpost[...])


def _mixer_prompt_kernel(x_ref, att_ref, up_ref, uph_ref, ug_ref, ugh_ref, gates_ref,
                         watt, wbd, pscale, wpo, cw, cb, lng, lnb, wco, wout, gpost,
                         o_ref, nconv_ref, pbuf, cplane):
    i = pl.program_id(1)
    p_rows = x_ref.shape[0]
    keep = (i > 0).astype(F32)

    pos = i * p_rows + lax.broadcasted_iota(jnp.int32, (p_rows, 1), 0)
    parts = []
    for gi, w in enumerate(POOL_WINDOWS):
        cs = slice(gi * POOL_GC, (gi + 1) * POOL_GC)
        ug = up_ref[:, cs]
        pbuf[gi, 0:POOL_HALO, :] = uph_ref[:, cs] * keep
        pbuf[gi, POOL_HALO:POOL_HALO + p_rows, :] = ug
        acc = ug
        for j in range(1, w):
            acc = acc + pbuf[gi, POOL_HALO - j:POOL_HALO - j + p_rows, :]
        cnt = jnp.minimum(w, pos + 1).astype(F32)
        parts.append(acc / cnt - ug)
    pooled = jnp.concatenate(parts, axis=1)

    nblk = CONV_W // HEAD_DIM
    for c in range(nblk):
        cs = slice(c * HEAD_DIM, (c + 1) * HEAD_DIM)
        cplane[c, 0:CONV_HALO, :] = ugh_ref[:, cs] * keep
        cplane[c, CONV_HALO:CONV_HALO + p_rows, :] = ug_ref[:, cs]
    off0 = CONV_HALO - CONV_STATE
    chunks = []
    for r0 in range(0, p_rows, CONV_CHUNK):
        cols = []
        for c in range(nblk):
            yc = None
            for j in range(CONV_K):
                win = cplane[c, off0 + j + r0:off0 + j + r0 + CONV_CHUNK, :]
                term = cw[j:j + 1, c * HEAD_DIM:(c + 1) * HEAD_DIM] * win
                yc = term if yc is None else yc + term
            cols.append(yc)
        chunks.append(jnp.concatenate(cols, axis=1))
    y = jnp.concatenate(chunks, axis=0)

    @pl.when(i == pl.num_programs(1) - 1)
    def _():
        for c in range(nblk):
            nconv_ref[:, c * HEAD_DIM:(c + 1) * HEAD_DIM] = cplane[c, p_rows:p_rows + CONV_HALO, :]

    o_ref[...] = _mix_tail(x_ref[...], att_ref[...], pooled, y, gates_ref[...],
                           (watt, wbd, pscale, wpo, cb, lng, lnb, wco, wout, gpost))


def _weight_specs(layer):
    def vec(width):
        return pl.BlockSpec((None, 1, width), lambda *_: (layer, 0, 0))

    def mat(rows, cols):
        return pl.BlockSpec((None, rows, cols), lambda *_: (layer, 0, 0))

    return vec, mat


def _mixer_prompt(x, att, upc, gates, wts, layer, batch, seq):
    t = x.shape[0]
    p = MIX_P
    tiles = seq // p
    vec, mat = _weight_specs(layer)

    def row(i_b, i_t):
        return i_b * tiles + i_t

    def halo(rows, colblk):
        per_tile = p // rows
        per_seq = seq // rows
        return pl.BlockSpec(
            (rows, POOL_W), lambda b, i: (jnp.maximum(b * per_seq + i * per_tile - 1, 0), colblk))

    in_specs = [
        pl.BlockSpec((p, D_MODEL), lambda b, i: (row(b, i), 0)),
        pl.BlockSpec((p, GROUP_W), lambda b, i: (row(b, i), 0)),
        pl.BlockSpec((p, POOL_W), lambda b, i: (row(b, i), 0)),
        halo(POOL_HALO, 0),
        pl.BlockSpec((p, CONV_W), lambda b, i: (row(b, i), 1)),
        halo(CONV_HALO, 1),
        pl.BlockSpec((p, GATE_W), lambda b, i: (row(b, i), 0)),
        mat(GROUP_W, D_MODEL), mat(POOL_W, POOL_W), vec(POOL_W), mat(POOL_W, D_MODEL),
        mat(CONV_K, CONV_W), vec(CONV_W), vec(CONV_W), vec(CONV_W), mat(CONV_W, D_MODEL),
        mat(D_MODEL, D_MODEL), vec(D_MODEL),
    ]
    return pl.pallas_call(
        _mixer_prompt_kernel,
        grid=(batch, tiles),
        in_specs=in_specs,
        out_specs=[
            pl.BlockSpec((p, D_MODEL), lambda b, i: (row(b, i), 0)),
            pl.BlockSpec((None, CONV_HALO, CONV_W), lambda b, i: (b, 0, 0)),
        ],
        out_shape=[
            jax.ShapeDtypeStruct((t, D_MODEL), F32),
            jax.ShapeDtypeStruct((batch, CONV_HALO, CONV_W), F32),
        ],
        scratch_shapes=[
            pltpu.VMEM((POOL_W // POOL_GC, POOL_HALO + p, POOL_GC), F32),
            pltpu.VMEM((CONV_W // HEAD_DIM, CONV_HALO + p, HEAD_DIM), F32),
        ],
        compiler_params=_params(("parallel", "arbitrary")),
        name="mixer_prompt",
    )(x, att, upc, upc, upc, upc, gates, *wts)


def _mixer_sample_kernel(x_ref, att_ref, up_ref, ug_ref, gates_ref, sp_ref, sc_ref,
                         watt, wbd, pscale, wpo, cw, cb, lng, lnb, wco, wout, gpost,
                         o_ref, npool_ref, nconv_ref):
    t_new = x_ref.shape[0]
    bb = x_ref.shape[1]

    def rows(ref):
        return jnp.concatenate([ref[t] for t in range(t_new)], axis=0)

    new_p = [up_ref[t] for t in range(t_new)]
    hist_p = [sp_ref[j] for j in range(POOL_STATE)] + new_p
    pooled = []
    for t in range(t_new):
        parts = []
        for gi, w in enumerate(POOL_WINDOWS):
            cs = slice(gi * POOL_GC, (gi + 1) * POOL_GC)
            acc = new_p[t][:, cs]
            for j in range(1, w):
                acc = acc + hist_p[POOL_STATE + t - j][:, cs]
            cnt = float(min(w, PAST_LEN + t + 1))
            parts.append(acc / cnt - new_p[t][:, cs])
        pooled.append(jnp.concatenate(parts, axis=1))
    for j in range(POOL_STATE):
        npool_ref[j] = hist_p[t_new + j]

    hist_c = [sc_ref[j] for j in range(CONV_STATE)] + [ug_ref[t] for t in range(t_new)]
    ys = []
    for t in range(t_new):
        y = cw[0:1, :] * hist_c[t]
        for j in range(1, CONV_K):
            y = y + cw[j:j + 1, :] * hist_c[t + j]
        ys.append(y)
    for j in range(CONV_STATE):
        nconv_ref[j] = hist_c[t_new + j]

    out = _mix_tail(rows(x_ref), rows(att_ref), jnp.concatenate(pooled, axis=0),
                    jnp.concatenate(ys, axis=0), rows(gates_ref),
                    (watt, wbd, pscale, wpo, cb, lng, lnb, wco, wout, gpost))
    for t in range(t_new):
        o_ref[t] = out[t * bb:(t + 1) * bb]


def _mixer_sample(x, att, upc, gates, sp, sc, wts, layer):
    t_new, nreq, _ = x.shape
    bb = SAMPLE_MIX_BB
    vec, mat = _weight_specs(layer)

    def act(width, colblk=0):
        return pl.BlockSpec((t_new, bb, width), lambda i: (0, i, colblk))

    in_specs = [
        act(D_MODEL), act(GROUP_W), act(POOL_W, 0), act(CONV_W, 1), act(GATE_W),
        pl.BlockSpec((None, POOL_STATE, bb, POOL_W), lambda i: (layer, 0, i, 0)),
        pl.BlockSpec((None, CONV_STATE, bb, CONV_W), lambda i: (layer, 0, i, 0)),
        mat(GROUP_W, D_MODEL), mat(POOL_W, POOL_W), vec(POOL_W), mat(POOL_W, D_MODEL),
        mat(CONV_K, CONV_W), vec(CONV_W), vec(CONV_W), vec(CONV_W), mat(CONV_W, D_MODEL),
        mat(D_MODEL, D_MODEL), vec(D_MODEL),
    ]
    return pl.pallas_call(
        _mixer_sample_kernel,
        grid=(nreq // bb,),
        in_specs=in_specs,
        out_specs=[
            act(D_MODEL),
            pl.BlockSpec((POOL_STATE, bb, POOL_W), lambda i: (0, i, 0)),
            pl.BlockSpec((CONV_STATE, bb, CONV_W), lambda i: (0, i, 0)),
        ],
        out_shape=[
            jax.ShapeDtypeStruct((t_new, nreq, D_MODEL), F32),
            jax.ShapeDtypeStruct((POOL_STATE, nreq, POOL_W), F32),
            jax.ShapeDtypeStruct((CONV_STATE, nreq, CONV_W), F32),
        ],
        compiler_params=_params(("parallel",)),
        name="mixer_sample",
    )(x, att, upc, upc, gates, sp, sc, *wts)


def _kv_pack_kernel(*refs, depth):
    ins, (o0, o1, o2) = refs[:6 * depth], refs[6 * depth:]
    layer, t, nt = pl.program_id(0), pl.program_id(2), pl.num_programs(2)
    t1 = o1.shape[0] // (KV_TILE * KV_ROWS)

    def pack(o_ref, k_ref, v_ref, row0=0):
        n = k_ref.shape[1]
        for j in range(HEADS):
            o_ref[pl.ds(row0 + j, n, stride=KV_ROWS), :] = k_ref[j]
            o_ref[pl.ds(row0 + HEADS + j, n, stride=KV_ROWS), :] = v_ref[j]

    for lyr in range(depth):
        k2, v2, k1, v1, k0, v0 = ins[6 * lyr:6 * lyr + 6]

        @pl.when(layer == lyr)
        def _(k2=k2, v2=v2, k1=k1, v1=v1, k0=k0, v0=v0):
            pack(o2, k2, v2)
            for s in range(t1):
                @pl.when(t == nt - t1 + s)
                def _(s=s):
                    pack(o1, k1, v1, s * KV_TILE * KV_ROWS)

            @pl.when(t == nt - 1)
            def _():
                pack(o0, k0, v0)


def _kv_pack(qkvs, batch, seq):
    depth = len(qkvs)
    w0, w1, w2 = (min(w, seq) for w in WINDOWS)
    assert w2 % KV_TILE == 0 and w1 % KV_TILE == 0 and seq % KV_TILE == 0 and seq % w0 == 0
    assert w0 <= KV_TILE and w2 >= w1
    nt, t1 = w2 // KV_TILE, w1 // KV_TILE
    per_seq = seq // KV_TILE

    def frozen(lyr, fn):
        def index_map(l, b, t):
            b = jnp.where(l == lyr, b, jnp.where(l < lyr, 0, batch - 1))
            t = jnp.where(l == lyr, t, jnp.where(l < lyr, 0, nt - 1))
            return fn(b, t)
        return index_map

    def col(sec, g):
        return sec * N_GROUPS + g

    in_specs, args = [], []
    for lyr in range(depth):
        for g, sec in ((2, 1), (2, 2), (1, 1), (1, 2), (0, 1), (0, 2)):
            if g == 2:
                spec = pl.BlockSpec((HEADS, KV_TILE, HEAD_DIM), frozen(
                    lyr, lambda b, t, c=col(sec, g): (c, b * per_seq + per_seq - nt + t, 0)))
            elif g == 1:
                spec = pl.BlockSpec((HEADS, KV_TILE, HEAD_DIM), frozen(
                    lyr, lambda b, t, c=col(sec, g):
                    (c, b * per_seq + per_seq - t1 + jnp.clip(t - (nt - t1), 0, t1 - 1), 0)))
            else:
                spec = pl.BlockSpec((HEADS, w0, HEAD_DIM), frozen(
                    lyr, lambda b, t, c=col(sec, g): (c, (b + 1) * (seq // w0) - 1, 0)))
            in_specs.append(spec)
            args.append(qkvs[lyr])

    def out(w):
        return (jax.ShapeDtypeStruct((depth, batch, w * KV_ROWS, HEAD_DIM), F32))

    o0, o1, o2 = pl.pallas_call(
        partial(_kv_pack_kernel, depth=depth),
        grid=(depth, batch, nt),
        in_specs=in_specs,
        out_specs=[
            pl.BlockSpec((None, None, w0 * KV_ROWS, HEAD_DIM), lambda l, b, t: (l, b, 0, 0)),
            pl.BlockSpec((None, None, w1 * KV_ROWS, HEAD_DIM), lambda l, b, t: (l, b, 0, 0)),
            pl.BlockSpec((None, None, KV_TILE * KV_ROWS, HEAD_DIM), lambda l, b, t: (l, b, t, 0)),
        ],
        out_shape=[out(w0), out(w1), out(w2)],
        compiler_params=_params(("arbitrary", "arbitrary", "arbitrary")),
        name="kv_pack",
    )(*args)
    return tuple(o.reshape(depth, batch, w, 2, HEADS, HEAD_DIM) for o, w in ((o0, w0), (o1, w1), (o2, w2)))


def _kv_rows(qkv, g, nreq, t_new):
    nh = N_GROUPS * HEADS
    kv = jnp.stack([qkv[sec * nh + g * HEADS:sec * nh + (g + 1) * HEADS] for sec in (1, 2)])
    return jnp.moveaxis(kv, 2, 0).reshape(nreq, t_new, 2, HEADS, HEAD_DIM)


def kernel(x_prompt, x_sample, cache_kv_w128, cache_kv_w512, cache_kv_w2048, state_pool, state_conv,
           w_in, w_att_o, w_pool_g, pool_scale, w_pool_o, conv_w, conv_b, ln_g, ln_b, w_conv_o, w_out,
           g_mix_pre, g_mix_post, w_ffn_in, w_ffn_down, g_ffn_pre, g_ffn_post):
    batch, seq, _ = x_prompt.shape
    nreq, t_new, _ = x_sample.shape
    depth = w_in.shape[0]

    w_in_b = w_in.astype(BF16)
    w_ffn_in_b = w_ffn_in.astype(BF16)
    w_ffn_down_b = w_ffn_down.astype(BF16)
    eye = jnp.eye(len(POOL_WINDOWS), dtype=F32)
    w_bd = (w_pool_g[:, :, :, None, :] * eye[None, :, None, :, None]).reshape(depth, POOL_W, POOL_W)

    def vec(a):
        return a.reshape(depth, 1, a.shape[-1])

    mix_w = (w_att_o.astype(BF16), w_bd.astype(BF16), vec(pool_scale), w_pool_o.astype(BF16),
             conv_w, vec(conv_b), vec(ln_g), vec(ln_b), w_conv_o.astype(BF16), w_out.astype(BF16),
             vec(g_mix_post))
    g_pre, g_fpre, g_fpost = vec(g_mix_pre), vec(g_ffn_pre), vec(g_ffn_post)

    c0 = cache_kv_w128.reshape(depth, nreq, cache_kv_w128.shape[2] * KV_ROWS, HEAD_DIM)
    c1 = cache_kv_w512.reshape(depth, nreq, cache_kv_w512.shape[2] * KV_ROWS, HEAD_DIM)
    d2 = DILATIONS[2]
    c2 = cache_kv_w2048.reshape(depth, nreq, cache_kv_w2048.shape[2] // d2, d2 * KV_ROWS, HEAD_DIM)
    sp_t = jnp.swapaxes(state_pool, 1, 2)
    sc_t = jnp.swapaxes(state_conv, 1, 2)

    xp = x_prompt.reshape(batch * seq, D_MODEL)
    xs = jnp.swapaxes(x_sample, 0, 1).reshape(t_new * nreq, D_MODEL)
    outs = {k: [] for k in ("ppool", "pconv", "skv0", "skv1", "skv2", "spool", "sconv")}
    qkvs = []

    for l in range(depth):
        qkv, upc, gates = _norm_proj(xp, g_pre, w_in_b, l)
        att = _attn_prompt(qkv, batch, seq)
        x1, nconv = _mixer_prompt(xp, att, upc, gates, mix_w, l, batch, seq)
        xp = _ffn(x1, g_fpre, g_fpost, w_ffn_in_b, w_ffn_down_b, l)
        qkvs.append(qkv)
        outs["ppool"].append(upc.reshape(batch, seq, UPC_W)[:, seq - POOL_STATE:, :POOL_W])
        outs["pconv"].append(nconv[:, CONV_HALO - CONV_STATE:])

        qkv_s, upc_s, gates_s = _norm_proj(xs, g_pre, w_in_b, l)
        qkv_rm = jnp.swapaxes(qkv_s.reshape(-1, t_new, nreq, HEAD_DIM), 1, 2)
        qkv_rm = qkv_rm.reshape(-1, nreq * t_new, HEAD_DIM)
        att_s = _attn_sample(qkv_rm, c0, c1, c2, l, t_new)
        att_s = jnp.swapaxes(att_s.reshape(nreq, t_new, GROUP_W), 0, 1)
        x1s, npool_s, nconv_s = _mixer_sample(
            xs.reshape(t_new, nreq, D_MODEL), att_s, upc_s.reshape(t_new, nreq, UPC_W),
            gates_s.reshape(t_new, nreq, GATE_W), sp_t, sc_t, mix_w, l)
        xs = _ffn(x1s.reshape(t_new * nreq, D_MODEL), g_fpre, g_fpost, w_ffn_in_b, w_ffn_down_b, l)
        for g in range(N_GROUPS):
            outs[f"skv{g}"].append(_kv_rows(qkv_rm, g, nreq, t_new))
        outs["spool"].append(jnp.swapaxes(npool_s, 0, 1))
        outs["sconv"].append(jnp.swapaxes(nconv_s, 0, 1))

    st = {k: jnp.stack(v) for k, v in outs.items()}
    pkv0, pkv1, pkv2 = _kv_pack(qkvs, batch, seq)
    return (xp.reshape(batch, seq, D_MODEL), jnp.swapaxes(xs.reshape(t_new, nreq, D_MODEL), 0, 1),
            pkv0, pkv1, pkv2, st["ppool"], st["pconv"],
            st["skv0"], st["skv1"], st["skv2"], st["spool"], st["sconv"])
```
